```python
import math
import jax, jax.numpy as jnp
from jax import lax
import numpy as np

D_MODEL = 1024
BATCH = 16
SEQ = 4096
DEPTH = 1

D_MIX = D_MODEL
D_SSM = D_MIX // 2
SSM_GROUP = 16
N_SSM_GROUPS = D_SSM // SSM_GROUP
SSM_STATE = 64
D_ATTN = D_MIX - D_SSM
N_HEADS = 8
QK_NOPE = 64
QK_ROPE = 32
V_HEAD = D_ATTN // N_HEADS
Q_LORA = 384
KV_LORA = 256
IN_COLS = D_SSM + Q_LORA + KV_LORA + QK_ROPE
D_FF = 4 * D_MODEL
ROPE_BASE = 10000.0
Q_BLOCK = 128
EPS = 1e-6
DT_MIN = 1e-3
DT_MAX = 1e-1
N_MOD = 6

kernel_name = "hymba_s5_mla_adaln_block"


def rmsnorm(x, g):
    xf = x.astype(jnp.float32)
    y = xf * lax.rsqrt(jnp.mean(xf * xf, axis=-1, keepdims=True) + EPS)
    return (y * g.astype(jnp.float32)).astype(x.dtype)


def rope_tables(positions):
    inv_freq = ROPE_BASE ** (-jnp.arange(0, QK_ROPE, 2, dtype=jnp.float32) / QK_ROPE)
    ang = positions.astype(jnp.float32)[..., None] * inv_freq
    return jnp.cos(ang), jnp.sin(ang)


def apply_rope(x, cos, sin):
    xf = x.astype(jnp.float32)
    x1, x2 = jnp.split(xf, 2, axis=-1)
    out = jnp.concatenate([x1 * cos - x2 * sin, x1 * sin + x2 * cos], axis=-1)
    return out.astype(x.dtype)


def s5_mixer(u, lam_re, lam_im, b_re, b_im, c_re, c_im, d, log_dt, w_glu):
    f32 = jnp.float32
    bsz, seq, _ = u.shape
    uf = u.astype(f32).reshape(bsz, seq, N_SSM_GROUPS, SSM_GROUP)
    lam = lax.complex(lam_re.astype(f32), lam_im.astype(f32))
    dt = jnp.exp(log_dt.astype(f32))[:, None]
    lam_bar = jnp.exp(lam * dt)
    b = lax.complex(b_re.astype(f32), b_im.astype(f32))
    b_bar = ((lam_bar - 1.0) / lam)[..., None] * b
    bu = jnp.einsum("bsgh,gph->bsgp", uf, b_bar)
    a = jnp.broadcast_to(lam_bar, (1, seq) + lam_bar.shape)

    def combine(left, right):
        a_l, b_l = left
        a_r, b_r = right
        return a_r * a_l, a_r * b_l + b_r

    _, states = lax.associative_scan(combine, (a, bu), axis=1)
    y = (jnp.einsum("bsgp,ghp->bsgh", jnp.real(states), c_re.astype(f32))
         - jnp.einsum("bsgp,ghp->bsgh", jnp.imag(states), c_im.astype(f32))
         + d.astype(f32) * uf)
    y = jax.nn.gelu(y).reshape(bsz, seq, D_SSM).astype(u.dtype)
    z = y @ w_glu
    return z[..., :D_SSM] * jax.nn.sigmoid(z[..., D_SSM:])


def causal_block_attention(q_nope, q_rope, k_nope, k_rope, v):
    bsz, seq = q_nope.shape[:2]
    n_blocks = seq // Q_BLOCK
    scale = (QK_NOPE + QK_ROPE) ** -0.5
    key_pos = jnp.arange(seq)

    def one_block(i):
        start = i * Q_BLOCK
        qn = lax.dynamic_slice_in_dim(q_nope, start, Q_BLOCK, axis=1)
        qr = lax.dynamic_slice_in_dim(q_rope, start, Q_BLOCK, axis=1)
        s = (jnp.einsum("bqhd,bkhd->bhqk", qn, k_nope)
             + jnp.einsum("bqhr,bkr->bhqk", qr, k_rope)).astype(jnp.float32) * scale
        q_pos = start + jnp.arange(Q_BLOCK)
        mask = key_pos[None, :] <= q_pos[:, None]
        s = jnp.where(mask, s, -jnp.inf)
        p = jax.nn.softmax(s, axis=-1).astype(v.dtype)
        return jnp.einsum("bhqk,bkhd->bqhd", p, v)

    out = lax.map(one_block, jnp.arange(n_blocks))
    return out.transpose(1, 0, 2, 3, 4).reshape(bsz, seq, N_HEADS, V_HEAD)


def mla_mixer(q_lat, kv_lat, k_rope, cos, sin, q_norm_g, w_uq, kv_norm_g, w_ukv):
    bsz, seq, _ = q_lat.shape
    q = (rmsnorm(q_lat, q_norm_g) @ w_uq).reshape(bsz, seq, N_HEADS, QK_NOPE + QK_ROPE)
    kv = (rmsnorm(kv_lat, kv_norm_g) @ w_ukv).reshape(bsz, seq, N_HEADS, QK_NOPE + V_HEAD)
    q_nope, q_rope = q[..., :QK_NOPE], q[..., QK_NOPE:]
    k_nope, v = kv[..., :QK_NOPE], kv[..., QK_NOPE:]
    q_rope = apply_rope(q_rope, cos[:, :, None, :], sin[:, :, None, :])
    k_rope = apply_rope(k_rope, cos, sin)
    out = causal_block_attention(q_nope, q_rope, k_nope, k_rope, v)
    return out.reshape(bsz, seq, D_ATTN)


def setup_inputs(seed: int = 0) -> dict:
    key = jax.random.key(seed)
    ks = jax.random.split(key, 32)
    f32 = jnp.float32
    L = DEPTH

    def nrm(k, shape, scale):
        return jax.random.normal(k, shape, f32) * scale

    def gain(k, shape):
        return 1.0 + 0.02 * jax.random.normal(k, shape, f32)

    x = jax.random.normal(ks[0], (BATCH, SEQ, D_MODEL), f32)
    c = jax.random.normal(ks[1], (BATCH, D_MODEL), f32)
    offset = jax.random.randint(ks[2], (BATCH, 1), 0, 2048, dtype=jnp.int32)
    positions = offset + jnp.arange(SEQ, dtype=jnp.int32)[None, :]

    n_idx = jnp.arange(SSM_STATE, dtype=f32)
    lam_re = -0.5 * jnp.exp(0.01 * jax.random.normal(ks[3], (L, N_SSM_GROUPS, SSM_STATE), f32))
    lam_im = math.pi * n_idx + 0.01 * jax.random.normal(ks[4], (L, N_SSM_GROUPS, SSM_STATE), f32)
    log_dt = jax.random.uniform(ks[5], (L, N_SSM_GROUPS), f32, math.log(DT_MIN), math.log(DT_MAX))

    return {
        "x": x,
        "c": c,
        "positions": positions,
        "ada_w": nrm(ks[6], (L, D_MODEL, N_MOD * D_MODEL), 0.5 * D_MODEL ** -0.5),
        "ada_b": nrm(ks[7], (L, N_MOD * D_MODEL), 0.02),
        "norm1_g": gain(ks[8], (L, D_MODEL)),
        "w_in": nrm(ks[9], (L, D_MODEL, IN_COLS), D_MODEL ** -0.5),
        "ssm_lambda_re": lam_re,
        "ssm_lambda_im": lam_im,
        "ssm_b_re": nrm(ks[10], (L, N_SSM_GROUPS, SSM_STATE, SSM_GROUP), (2 * SSM_GROUP) ** -0.5),
        "ssm_b_im": nrm(ks[11], (L, N_SSM_GROUPS, SSM_STATE, SSM_GROUP), (2 * SSM_GROUP) ** -0.5),
        "ssm_c_re": nrm(ks[12], (L, N_SSM_GROUPS, SSM_GROUP, SSM_STATE), (2 * SSM_STATE) ** -0.5),
        "ssm_c_im": nrm(ks[13], (L, N_SSM_GROUPS, SSM_GROUP, SSM_STATE), (2 * SSM_STATE) ** -0.5),
        "ssm_d": nrm(ks[14], (L, N_SSM_GROUPS, SSM_GROUP), 1.0),
        "ssm_log_dt": log_dt,
        "w_glu": nrm(ks[15], (L, D_SSM, 2 * D_SSM), D_SSM ** -0.5),
        "q_norm_g": gain(ks[16], (L, Q_LORA)),
        "w_uq": nrm(ks[17], (L, Q_LORA, N_HEADS * (QK_NOPE + QK_ROPE)), Q_LORA ** -0.5),
        "kv_norm_g": gain(ks[18], (L, KV_LORA)),
        "w_ukv": nrm(ks[19], (L, KV_LORA, N_HEADS * (QK_NOPE + V_HEAD)), KV_LORA ** -0.5),
        "ssm_out_g": gain(ks[20], (L, D_SSM)),
        "attn_out_g": gain(ks[21], (L, D_ATTN)),
        "w_out": nrm(ks[22], (L, D_MIX, D_MODEL), D_MIX ** -0.5),
        "norm2_g": gain(ks[23], (L, D_MODEL)),
        "w_ff1": nrm(ks[24], (L, D_MODEL, D_FF), D_MODEL ** -0.5),
        "w_ff2": nrm(ks[25], (L, D_FF, D_MODEL), D_FF ** -0.5),
        "final_ada_w": nrm(ks[26], (D_MODEL, 2 * D_MODEL), 0.5 * D_MODEL ** -0.5),
        "final_ada_b": nrm(ks[27], (2 * D_MODEL,), 0.02),
        "final_norm_g": gain(ks[28], (D_MODEL,)),
    }


def reference(x, c, positions, ada_w, ada_b, norm1_g, w_in, ssm_lambda_re, ssm_lambda_im,
              ssm_b_re, ssm_b_im, ssm_c_re, ssm_c_im, ssm_d, ssm_log_dt, w_glu,
              q_norm_g, w_uq, kv_norm_g, w_ukv, ssm_out_g, attn_out_g, w_out,
              norm2_g, w_ff1, w_ff2, final_ada_w, final_ada_b, final_norm_g):
    cond = jax.nn.silu(c)
    cos, sin = rope_tables(positions)
    s1 = D_SSM
    s2 = s1 + Q_LORA
    s3 = s2 + KV_LORA
    for l in range(DEPTH):
        mod = (cond @ ada_w[l] + ada_b[l])[:, None, :]
        shift1, scale1, gate1, shift2, scale2, gate2 = jnp.split(mod, N_MOD, axis=-1)

        h = rmsnorm(x, norm1_g[l]) * (1.0 + scale1) + shift1
        proj = h @ w_in[l]
        u = proj[..., :s1]
        q_lat = proj[..., s1:s2]
        kv_lat = proj[..., s2:s3]
        k_rope = proj[..., s3:]
        y_ssm = s5_mixer(u, ssm_lambda_re[l], ssm_lambda_im[l], ssm_b_re[l], ssm_b_im[l],
                         ssm_c_re[l], ssm_c_im[l], ssm_d[l], ssm_log_dt[l], w_glu[l])
        y_attn = mla_mixer(q_lat, kv_lat, k_rope, cos, sin, q_norm_g[l], w_uq[l],
                           kv_norm_g[l], w_ukv[l])
        y = jnp.concatenate([rmsnorm(y_ssm, ssm_out_g[l]), rmsnorm(y_attn, attn_out_g[l])], axis=-1)
        x = x + gate1 * (y @ w_out[l])

        h = rmsnorm(x, norm2_g[l]) * (1.0 + scale2) + shift2
        ff = jnp.square(jax.nn.relu(h @ w_ff1[l])) @ w_ff2[l]
        x = x + gate2 * ff

    fmod = (cond @ final_ada_w + final_ada_b)[:, None, :]
    fshift, fscale = jnp.split(fmod, 2, axis=-1)
    return rmsnorm(x, final_norm_g) * (1.0 + fscale) + fshift
```

```python
import functools

import jax
import jax.numpy as jnp
from jax import lax
from jax.experimental import pallas as pl
from jax.experimental.pallas import tpu as pltpu

F32 = jnp.float32
BF16 = jnp.bfloat16

SSM_GROUP = 16
SSM_STATE = 64
N_HEADS = 8
QK_NOPE = 64
QK_ROPE = 32
V_HEAD = 64
ROPE_BASE = 10000.0
EPS = 1e-6
LANE = 128
HEAD_PAD = LANE
MASK_VALUE = -1e30
VMEM_LIMIT = 56 * 1024 * 1024


def _rms(x):
    return x * lax.rsqrt(jnp.mean(x * x, axis=-1, keepdims=True) + EPS)


def _dot(a, b):
    return jnp.dot(a, b, preferred_element_type=F32)


def _resident(shape):
    zeros = (0,) * len(shape)
    return pl.BlockSpec(shape, lambda *_: zeros, pipeline_mode=pl.Buffered(1))


def _params(n_grid_dims):
    return pltpu.CompilerParams(dimension_semantics=("arbitrary",) * n_grid_dims,
                                vmem_limit_bytes=VMEM_LIMIT)


def _mod_body(c_ref, w_ref, b_ref, o_ref):
    c = c_ref[...]
    cond = c * jax.nn.sigmoid(c)
    o_ref[...] = jnp.dot(cond, w_ref[...], preferred_element_type=F32,
                         precision=lax.Precision.HIGHEST) + b_ref[...]


def _modulation(c, w, b, block_n=1024):
    bsz, d = c.shape
    n = w.shape[1]
    return pl.pallas_call(
        _mod_body,
        grid=(n // block_n,),
        in_specs=[pl.BlockSpec((bsz, d), lambda j: (0, 0)),
                  pl.BlockSpec((d, block_n), lambda j: (0, j)),
                  pl.BlockSpec((1, block_n), lambda j: (0, j))],
        out_specs=pl.BlockSpec((bsz, block_n), lambda j: (0, j)),
        out_shape=jax.ShapeDtypeStruct((bsz, n), F32),
        compiler_params=_params(1),
    )(c, w, b.reshape(1, n))


def _disc_body(lre_ref, lim_ref, ldt_ref, bre_ref, bim_ref, are_ref, aim_ref, obre_ref, obim_ref):
    lre = lre_ref[...]
    lim = lim_ref[...]
    dt = jnp.exp(ldt_ref[...])
    mag = jnp.exp(lre * dt)
    are = mag * jnp.cos(lim * dt)
    aim = mag * jnp.sin(lim * dt)
    are_ref[...] = are
    aim_ref[...] = aim
    nre = are - 1.0
    den = lre * lre + lim * lim
    cre = (nre * lre + aim * lim) / den
    cim = (aim * lre - nre * lim) / den
    bre = bre_ref[...]
    bim = bim_ref[...]
    obre_ref[...] = cre * bre - cim * bim
    obim_ref[...] = cre * bim + cim * bre


def _discretize(lam_re, lam_im, log_dt, b_re_t, b_im_t):
    g, p = lam_re.shape
    h = b_re_t.shape[1]
    are, aim, bbar_re, bbar_im = pl.pallas_call(
        _disc_body,
        out_shape=(jax.ShapeDtypeStruct((g, 1, p), F32), jax.ShapeDtypeStruct((g, 1, p), F32),
                   jax.ShapeDtypeStruct((g, h, p), F32), jax.ShapeDtypeStruct((g, h, p), F32)),
    )(lam_re.reshape(g, 1, p), lam_im.reshape(g, 1, p), log_dt.reshape(g, 1, 1), b_re_t, b_im_t)
    return are.reshape(g, p), aim.reshape(g, p), bbar_re, bbar_im


def _rope_body(pos_ref, f_ref, cos_ref, sin_ref):
    ang = pos_ref[...] * f_ref[...]
    cos_ref[...] = jnp.cos(ang)
    sin_ref[...] = jnp.sin(ang)


def _rope_tables(posf, inv_freq):
    bsz, seq = posf.shape
    nf = inv_freq.shape[0]
    out = jax.ShapeDtypeStruct((nf, bsz, seq), F32)
    return pl.pallas_call(
        _rope_body,
        grid=(nf,),
        in_specs=[pl.BlockSpec((bsz, seq), lambda f: (0, 0)),
                  pl.BlockSpec((None, 1, 1), lambda f: (f, 0, 0))],
        out_specs=(pl.BlockSpec((None, bsz, seq), lambda f: (f, 0, 0)),
                   pl.BlockSpec((None, bsz, seq), lambda f: (f, 0, 0))),
        out_shape=(out, out),
        compiler_params=_params(1),
    )(posf, inv_freq.reshape(nf, 1, 1))


def _inproj_body(x_ref, mod_ref, g1_ref, win_ref, qg_ref, wuq_ref, kvg_ref, wk_ref, wv_ref,
                 t1_ref, t2_ref, u_ref, q_ref, k_ref, v_ref, *, d_model, d_ssm, q_lora, kv_lora, scale):
    x = x_ref[...]
    mod = mod_ref[...]
    shift1 = mod[:, :d_model]
    scale1 = mod[:, d_model:2 * d_model]
    h = _rms(x) * g1_ref[...] * (1.0 + scale1) + shift1
    proj = _dot(h.astype(BF16), win_ref[...])
    u_ref[...] = proj[:, :d_ssm].astype(BF16)

    s2 = d_ssm + q_lora
    s3 = s2 + kv_lora
    qn = (_rms(proj[:, d_ssm:s2]) * qg_ref[...]).astype(BF16)
    q = _dot(qn, wuq_ref[...])
    kvn = (_rms(proj[:, s2:s3]) * kvg_ref[...]).astype(BF16)
    kn = _dot(kvn, wk_ref[...])
    v_ref[...] = _dot(kvn, wv_ref[...]).astype(BF16)

    t1 = t1_ref[...]
    t2 = t2_ref[...]
    kr = proj[:, s3:s3 + HEAD_PAD]
    kr = kr * t1 + pltpu.roll(kr, HEAD_PAD - QK_ROPE, 1) * t2
    for hh in range(N_HEADS):
        sl = slice(hh * HEAD_PAD, (hh + 1) * HEAD_PAD)
        qh = q[:, sl]
        qh = qh * t1 + pltpu.roll(qh, HEAD_PAD - QK_ROPE, 1) * t2
        q_ref[:, sl] = (qh * scale).astype(BF16)
        k_ref[:, sl] = (kn[:, sl] + kr).astype(BF16)


def _in_proj(x, mod3, g1, win, qg, wuq, kvg, wk, wv, t1, t2, *, tm, d_ssm, q_lora, kv_lora):
    bsz, seq, d = x.shape
    nt = seq // tm
    hp = N_HEADS * HEAD_PAD
    scale = float((QK_NOPE + QK_ROPE) ** -0.5)
    const = lambda b, i: (0, 0)
    body = functools.partial(_inproj_body, d_model=d, d_ssm=d_ssm, q_lora=q_lora, kv_lora=kv_lora,
                             scale=scale)
    return pl.pallas_call(
        body,
        grid=(bsz, nt),
        in_specs=[pl.BlockSpec((None, tm, d), lambda b, i: (b, i, 0)),
                  pl.BlockSpec((None, 1, mod3.shape[2]), lambda b, i: (b, 0, 0)),
                  pl.BlockSpec(g1.shape, const),
                  pl.BlockSpec(win.shape, const),
                  pl.BlockSpec(qg.shape, const),
                  pl.BlockSpec(wuq.shape, const),
                  pl.BlockSpec(kvg.shape, const),
                  pl.BlockSpec(wk.shape, const),
                  pl.BlockSpec(wv.shape, const),
                  pl.BlockSpec((None, tm, HEAD_PAD), lambda b, i: (b, i, 0)),
                  pl.BlockSpec((None, tm, HEAD_PAD), lambda b, i: (b, i, 0))],
        out_specs=(pl.BlockSpec((tm, d_ssm), lambda b, i: (i, b)),
                   pl.BlockSpec((None, tm, hp), lambda b, i: (b, i, 0)),
                   pl.BlockSpec((None, tm, hp), lambda b, i: (b, i, 0)),
                   pl.BlockSpec((None, tm, N_HEADS * V_HEAD), lambda b, i: (b, i, 0))),
        out_shape=(jax.ShapeDtypeStruct((seq, bsz * d_ssm), BF16),
                   jax.ShapeDtypeStruct((bsz, seq, hp), BF16),
                   jax.ShapeDtypeStruct((bsz, seq, hp), BF16),
                   jax.ShapeDtypeStruct((bsz, seq, N_HEADS * V_HEAD), BF16)),
        compiler_params=_params(2),
    )(x, mod3, g1, win, qg, wuq, kvg, wk, wv, t1, t2)


CHAINS_PER_LOOP = 4


def _s5_body(u_ref, bblk_ref, are_ref, aim_ref, cblk_ref, d_ref, wglu_ref, g_ref, y_ref,
             bu_scr, x_scr, st_scr, *, lt, bsz, d_ssm):
    rows = lt * bsz
    n_kt = bblk_ref.shape[0]
    kt_in = bblk_ref.shape[1]
    kt_st = bblk_ref.shape[2]
    pair = 2 * LANE

    @pl.when(pl.program_id(0) == 0)
    def _():
        st_scr[...] = jnp.zeros_like(st_scr)

    u = u_ref[...].reshape(rows, d_ssm)
    for kt in range(n_kt):
        bu_scr[:, kt * kt_st:(kt + 1) * kt_st] = _dot(u[:, kt * kt_in:(kt + 1) * kt_in], bblk_ref[kt])

    n_pairs = (n_kt * kt_st) // pair
    for m0 in range(0, n_pairs, CHAINS_PER_LOOP):
        ms = range(m0, m0 + CHAINS_PER_LOOP)
        ar = [jnp.broadcast_to(are_ref[m:m + 1, :], (bsz, LANE)) for m in ms]
        ai = [jnp.broadcast_to(aim_ref[m:m + 1, :], (bsz, LANE)) for m in ms]
        init = tuple((st_scr[:, m * pair:m * pair + LANE], st_scr[:, m * pair + LANE:(m + 1) * pair])
                     for m in ms)

        def step(t, carry, ms=ms, ar=ar, ai=ai):
            r0 = pl.multiple_of(t * bsz, bsz)
            out = []
            for j, m in enumerate(ms):
                xr, xi = carry[j]
                c0 = m * pair
                br = bu_scr[pl.ds(r0, bsz), c0:c0 + LANE]
                bi = bu_scr[pl.ds(r0, bsz), c0 + LANE:c0 + pair]
                nr = ar[j] * xr - ai[j] * xi + br
                ni = ar[j] * xi + ai[j] * xr + bi
                x_scr[pl.ds(r0, bsz), c0:c0 + LANE] = nr.astype(BF16)
                x_scr[pl.ds(r0, bsz), c0 + LANE:c0 + pair] = ni.astype(BF16)
                out.append((nr, ni))
            return tuple(out)

        fin = lax.fori_loop(0, lt, step, init)
        for j, m in enumerate(ms):
            st_scr[:, m * pair:m * pair + LANE] = fin[j][0]
            st_scr[:, m * pair + LANE:(m + 1) * pair] = fin[j][1]

    ys = [_dot(x_scr[:, kt * kt_st:(kt + 1) * kt_st], cblk_ref[kt]) for kt in range(n_kt)]
    y = jnp.concatenate(ys, axis=1) + d_ref[...] * u.astype(F32)
    y = jax.nn.gelu(y)
    z = _dot(y.astype(BF16), wglu_ref[...])
    o = z[:, :d_ssm] * jax.nn.sigmoid(z[:, d_ssm:])
    o = _rms(o) * g_ref[...]
    y_ref[...] = o.astype(BF16).reshape(lt, bsz, d_ssm)


def _s5(u_tm, bblk, are, aim, cblk, dvec, wglu, g, *, lt):
    seq, bsz, d_ssm = u_tm.shape
    n_state = bblk.shape[0] * bblk.shape[2]
    rows = lt * bsz
    c2 = lambda i: (0, 0)
    c3 = lambda i: (0, 0, 0)
    body = functools.partial(_s5_body, lt=lt, bsz=bsz, d_ssm=d_ssm)
    return pl.pallas_call(
        body,
        grid=(seq // lt,),
        in_specs=[pl.BlockSpec((lt, bsz, d_ssm), lambda i: (i, 0, 0)),
                  pl.BlockSpec(bblk.shape, c3),
                  pl.BlockSpec(are.shape, c2),
                  pl.BlockSpec(aim.shape, c2),
                  pl.BlockSpec(cblk.shape, c3),
                  pl.BlockSpec(dvec.shape, c2),
                  pl.BlockSpec(wglu.shape, c2),
                  pl.BlockSpec(g.shape, c2)],
        out_specs=pl.BlockSpec((lt, bsz, d_ssm), lambda i: (i, 0, 0)),
        out_shape=jax.ShapeDtypeStruct((seq, bsz, d_ssm), BF16),
        scratch_shapes=[pltpu.VMEM((rows, n_state), F32),
                        pltpu.VMEM((rows, n_state), BF16),
                        pltpu.VMEM((bsz, n_state), F32)],
        compiler_params=_params(1),
    )(u_tm, bblk, are, aim, cblk, dvec, wglu, g)


def _attn_body(q_ref, k_ref, v_ref, g_ref, o_ref, m_scr, l_scr, acc_scr, o_scr, *, tq):
    qi = pl.program_id(1)
    row = lax.broadcasted_iota(jnp.int32, (tq, tq), 0)
    col = lax.broadcasted_iota(jnp.int32, (tq, tq), 1)
    causal = col <= row
    contract_last = (((1,), (1,)), ((), ()))

    for hh in range(N_HEADS):
        q_h = q_ref[:, hh * HEAD_PAD:(hh + 1) * HEAD_PAD]
        ksl = slice(hh * HEAD_PAD, (hh + 1) * HEAD_PAD)
        vsl = slice((hh // 2) * LANE, (hh // 2 + 1) * LANE)

        def scores(k0):
            k_h = k_ref[pl.ds(k0, tq), ksl]
            return lax.dot_general(q_h, k_h, contract_last, preferred_element_type=F32)

        k0 = pl.multiple_of(qi * tq, tq)
        s = jnp.where(causal, scores(k0), MASK_VALUE)
        m0 = jnp.max(s, axis=-1, keepdims=True)
        p = jnp.exp(s - m0)
        m_scr[...] = m0
        l_scr[...] = jnp.sum(p, axis=-1, keepdims=True)
        acc_scr[...] = _dot(p.astype(BF16), v_ref[pl.ds(k0, tq), vsl])

        def kv_step(ki, carry):
            kk = pl.multiple_of(ki * tq, tq)
            s = scores(kk)
            m_prev = m_scr[...]
            m_new = jnp.maximum(m_prev, jnp.max(s, axis=-1, keepdims=True))
            alpha = jnp.exp(m_prev - m_new)
            p = jnp.exp(s - m_new)
            l_scr[...] = alpha * l_scr[...] + jnp.sum(p, axis=-1, keepdims=True)
            acc_scr[...] = alpha * acc_scr[...] + _dot(p.astype(BF16), v_ref[pl.ds(kk, tq), vsl])
            m_scr[...] = m_new
            return carry

        lax.fori_loop(0, qi, kv_step, 0)
        o_scr[:, hh * LANE:(hh + 1) * LANE] = acc_scr[...] / l_scr[...]

    lane = lax.broadcasted_iota(jnp.int32, (tq, LANE), 1)
    first_half = lane < V_HEAD
    outs = []
    for j in range(N_HEADS // 2):
        even = o_scr[:, (2 * j) * LANE:(2 * j + 1) * LANE]
        odd = o_scr[:, (2 * j + 1) * LANE:(2 * j + 2) * LANE]
        outs.append(jnp.where(first_half, even, odd))
    o = jnp.concatenate(outs, axis=1)
    o_ref[...] = (_rms(o) * g_ref[...]).astype(BF16)


def _attention(q, k, v, g, *, tq):
    bsz, seq, hp = q.shape
    dv = v.shape[2]
    body = functools.partial(_attn_body, tq=tq)
    return pl.pallas_call(
        body,
        grid=(bsz, seq // tq),
        in_specs=[pl.BlockSpec((None, tq, hp), lambda b, i: (b, i, 0)),
                  pl.BlockSpec((None, seq, hp), lambda b, i: (b, 0, 0)),
                  pl.BlockSpec((None, seq, dv), lambda b, i: (b, 0, 0)),
                  pl.BlockSpec(g.shape, lambda b, i: (0, 0))],
        out_specs=pl.BlockSpec((None, tq, dv), lambda b, i: (b, i, 0)),
        out_shape=jax.ShapeDtypeStruct((bsz, seq, dv), BF16),
        scratch_shapes=[pltpu.VMEM((tq, 1), F32),
                        pltpu.VMEM((tq, 1), F32),
                        pltpu.VMEM((tq, LANE), F32),
                        pltpu.VMEM((tq, N_HEADS * LANE), F32)],
        compiler_params=_params(2),
    )(q, k, v, g)


FF_CHUNK = 1024


def _outffn_body(x_ref, ys_ref, ya_ref, mod_ref, fmod_ref, wos_ref, woa_ref, g2_ref, w1_ref, w2_ref,
                 gf_ref, o_ref, *, d_model):
    d = d_model
    x = x_ref[...]
    mod = mod_ref[...]
    gate1 = mod[:, 2 * d:3 * d]
    shift2 = mod[:, 3 * d:4 * d]
    scale2 = mod[:, 4 * d:5 * d]
    gate2 = mod[:, 5 * d:6 * d]
    mix = _dot(ys_ref[...], wos_ref[...]) + _dot(ya_ref[...], woa_ref[...])
    x1 = x + gate1 * mix
    h = (_rms(x1) * g2_ref[...] * (1.0 + scale2) + shift2).astype(BF16)
    d_ff = w1_ref.shape[1]
    ff = jnp.zeros_like(x1)
    for c0 in range(0, d_ff, FF_CHUNK):
        a = jnp.maximum(_dot(h, w1_ref[:, c0:c0 + FF_CHUNK]), 0.0)
        ff = ff + _dot((a * a).astype(BF16), w2_ref[c0:c0 + FF_CHUNK, :])
    x2 = x1 + gate2 * ff
    fmod = fmod_ref[...]
    fshift = fmod[:, :d]
    fscale = fmod[:, d:2 * d]
    o_ref[...] = _rms(x2) * gf_ref[...] * (1.0 + fscale) + fshift


def _out_ffn(x, ys2d, ya, mod3, fmod3, wos, woa, g2, w1, w2, gf, *, tm):
    bsz, seq, d = x.shape
    d_half = ya.shape[2]
    body = functools.partial(_outffn_body, d_model=d)
    return pl.pallas_call(
        body,
        grid=(bsz, seq // tm),
        in_specs=[pl.BlockSpec((None, tm, d), lambda b, i: (b, i, 0)),
                  pl.BlockSpec((tm, d_half), lambda b, i: (i, b)),
                  pl.BlockSpec((None, tm, d_half), lambda b, i: (b, i, 0)),
                  pl.BlockSpec((None, 1, mod3.shape[2]), lambda b, i: (b, 0, 0)),
                  pl.BlockSpec((None, 1, fmod3.shape[2]), lambda b, i: (b, 0, 0)),
                  _resident(wos.shape),
                  _resident(woa.shape),
                  _resident(g2.shape),
                  _resident(w1.shape),
                  _resident(w2.shape),
                  _resident(gf.shape)],
        out_specs=pl.BlockSpec((None, tm, d), lambda b, i: (b, i, 0)),
        out_shape=jax.ShapeDtypeStruct((bsz, seq, d), F32),
        compiler_params=_params(2),
    )(x, ys2d, ya, mod3, fmod3, wos, woa, g2, w1, w2, gf)


def _pad_heads(w, head_in, pieces):
    k = w.shape[0]
    w3 = w.reshape(k, N_HEADS, head_in)
    cols = []
    for piece in pieces:
        if isinstance(piece, int):
            cols.append(jnp.zeros((k, N_HEADS, piece), w.dtype))
        else:
            cols.append(piece(w3))
    out = jnp.concatenate(cols, axis=2)
    assert out.shape[2] == HEAD_PAD
    return out.reshape(k, N_HEADS * HEAD_PAD)


def _rot_half(wr):
    half = QK_ROPE // 2
    return jnp.concatenate([-wr[..., half:], wr[..., :half]], axis=-1)


def _s5_block_weights(bbar_re, bbar_im, c_re, c_im):
    g, h, p = bbar_re.shape
    n_pair = g // 2
    eye_m = jnp.eye(n_pair, dtype=F32)
    eye_g = jnp.eye(2, dtype=F32)
    bb = jnp.stack([bbar_re, bbar_im], axis=2).reshape(n_pair, 2, h, 2, p)
    b_full = jnp.einsum("mghrp,mn,gk->mghnrkp", bb, eye_m, eye_g).reshape(g * h, g * 2 * p)
    cc = jnp.stack([c_re, -c_im], axis=2).reshape(n_pair, 2, h, 2, p)
    c_full = jnp.einsum("mghrp,mn,gk->nrkpmgh", cc, eye_m, eye_g).reshape(g * 2 * p, g * h)
    n_kt = 2
    kin = (g * h) // n_kt
    kst = (g * 2 * p) // n_kt
    bblk = jnp.stack([b_full[kt * kin:(kt + 1) * kin, kt * kst:(kt + 1) * kst] for kt in range(n_kt)])
    cblk = jnp.stack([c_full[kt * kst:(kt + 1) * kst, kt * kin:(kt + 1) * kin] for kt in range(n_kt)])
    return bblk.astype(BF16), cblk.astype(BF16)


def _forward(x, c, positions, ada_w, ada_b, norm1_g, w_in, ssm_lambda_re, ssm_lambda_im,
             ssm_b_re, ssm_b_im, ssm_c_re, ssm_c_im, ssm_d, ssm_log_dt, w_glu,
             q_norm_g, w_uq, kv_norm_g, w_ukv, ssm_out_g, attn_out_g, w_out,
             norm2_g, w_ff1, w_ff2, final_ada_w, final_ada_b, final_norm_g, *, tm, tq, lt):
    bsz, seq, d = x.shape
    depth = ada_w.shape[0]
    assert depth == 1, "the fused epilogue applies the final norm right after the only layer"
    d_ssm = w_glu.shape[1]
    q_lora = w_uq.shape[1]
    kv_lora = w_ukv.shape[1]
    s2 = d_ssm + q_lora
    s3 = s2 + kv_lora

    inv_freq = ROPE_BASE ** (-jnp.arange(0, QK_ROPE, 2, dtype=F32) / QK_ROPE)
    cos_t, sin_t = _rope_tables(positions.astype(F32), inv_freq)
    cos_t = jnp.transpose(cos_t, (1, 2, 0))
    sin_t = jnp.transpose(sin_t, (1, 2, 0))
    ones = jnp.ones((bsz, seq, QK_NOPE), F32)
    zeros_n = jnp.zeros((bsz, seq, QK_NOPE), F32)
    zeros_r = jnp.zeros((bsz, seq, HEAD_PAD - QK_NOPE - QK_ROPE), F32)
    t1 = jnp.concatenate([ones, cos_t, cos_t, zeros_r], axis=-1)
    t2 = jnp.concatenate([zeros_n, sin_t, sin_t, zeros_r], axis=-1)

    fmod = _modulation(c, final_ada_w, final_ada_b)
    fmod3 = fmod.reshape(bsz, 1, fmod.shape[1])

    for l in range(depth):
        mod = _modulation(c, ada_w[l], ada_b[l])
        mod3 = mod.reshape(bsz, 1, mod.shape[1])

        wi = w_in[l]
        wkr = wi[:, s3:]
        win = jnp.concatenate([wi[:, :s3], jnp.zeros((d, QK_NOPE), F32), wkr, _rot_half(wkr)],
                              axis=1).astype(BF16)
        hq = QK_NOPE + QK_ROPE
        wuq = _pad_heads(w_uq[l], hq, [lambda w3: w3[..., :hq],
                                       lambda w3: _rot_half(w3[..., QK_NOPE:])]).astype(BF16)
        wk = _pad_heads(w_ukv[l], QK_NOPE + V_HEAD,
                        [lambda w3: w3[..., :QK_NOPE], HEAD_PAD - QK_NOPE]).astype(BF16)
        wv = w_ukv[l].reshape(kv_lora, N_HEADS, QK_NOPE + V_HEAD)[..., QK_NOPE:]
        wv = wv.reshape(kv_lora, N_HEADS * V_HEAD).astype(BF16)

        u2d, q, k, v = _in_proj(x, mod3, norm1_g[l].reshape(1, d), win, q_norm_g[l].reshape(1, q_lora),
                                wuq, kv_norm_g[l].reshape(1, kv_lora), wk, wv, t1, t2,
                                tm=tm, d_ssm=d_ssm, q_lora=q_lora, kv_lora=kv_lora)

        are, aim, bbar_re, bbar_im = _discretize(
            ssm_lambda_re[l], ssm_lambda_im[l], ssm_log_dt[l],
            jnp.transpose(ssm_b_re[l], (0, 2, 1)), jnp.transpose(ssm_b_im[l], (0, 2, 1)))
        bblk, cblk = _s5_block_weights(bbar_re, bbar_im, ssm_c_re[l], ssm_c_im[l])
        n_pair = are.shape[0] // 2
        ys_tm = _s5(u2d.reshape(seq, bsz, d_ssm), bblk, are.reshape(n_pair, LANE), aim.reshape(n_pair, LANE),
                    cblk, ssm_d[l].reshape(1, d_ssm), w_glu[l].astype(BF16), ssm_out_g[l].reshape(1, d_ssm),
                    lt=lt)

        ya = _attention(q, k, v, attn_out_g[l].reshape(1, -1), tq=tq)

        wo = w_out[l].astype(BF16)
        out = _out_ffn(x, ys_tm.reshape(seq, bsz * d_ssm), ya, mod3, fmod3, wo[:d_ssm], wo[d_ssm:],
                       norm2_g[l].reshape(1, d), w_ff1[l].astype(BF16), w_ff2[l].astype(BF16),
                       final_norm_g.reshape(1, d), tm=tm)
        x = out
    return x


def kernel(x, c, positions, ada_w, ada_b, norm1_g, w_in, ssm_lambda_re, ssm_lambda_im, ssm_b_re, ssm_b_im, ssm_c_re, ssm_c_im, ssm_d, ssm_log_dt, w_glu, q_norm_g, w_uq, kv_norm_g, w_ukv, ssm_out_g, attn_out_g, w_out, norm2_g, w_ff1, w_ff2, final_ada_w, final_ada_b, final_norm_g):
    return _forward(x, c, positions, ada_w, ada_b, norm1_g, w_in, ssm_lambda_re, ssm_lambda_im,
                    ssm_b_re, ssm_b_im, ssm_c_re, ssm_c_im, ssm_d, ssm_log_dt, w_glu,
                    q_norm_g, w_uq, kv_norm_g, w_ukv, ssm_out_g, attn_out_g, w_out,
                    norm2_g, w_ff1, w_ff2, final_ada_w, final_ada_b, final_norm_g,
                    tm=512, tq=512, lt=32)
```

```python
import functools
import math

import jax
import jax.numpy as jnp
from jax import lax
from jax.experimental import pallas as pl
from jax.experimental.pallas import tpu as pltpu

F32 = jnp.float32
BF16 = jnp.bfloat16

SSM_GROUP = 16
SSM_STATE = 64
N_HEADS = 8
QK_NOPE = 64
QK_ROPE = 32
V_HEAD = 64
ROPE_BASE = 10000.0
EPS = 1e-6
LANE = 128
HEAD_PAD = LANE
MASK_VALUE = -1e30
VMEM_LIMIT = 56 * 1024 * 1024


def _rms(x):
    return x * lax.rsqrt(jnp.mean(x * x, axis=-1, keepdims=True) + EPS)


def _dot(a, b):
    return jnp.dot(a, b, preferred_element_type=F32)


def _resident(shape):
    zeros = (0,) * len(shape)
    return pl.BlockSpec(shape, lambda *_: zeros, pipeline_mode=pl.Buffered(1))


def _params(n_grid_dims):
    return pltpu.CompilerParams(dimension_semantics=("arbitrary",) * n_grid_dims,
                                vmem_limit_bytes=VMEM_LIMIT)


def _mod_body(c_ref, w_ref, b_ref, o_ref):
    c = c_ref[...]
    cond = c * jax.nn.sigmoid(c)
    o_ref[...] = jnp.dot(cond, w_ref[...], preferred_element_type=F32,
                         precision=lax.Precision.HIGHEST) + b_ref[...]


def _modulation(c, w, b, block_n=1024):
    bsz, d = c.shape
    n = w.shape[1]
    return pl.pallas_call(
        _mod_body,
        grid=(n // block_n,),
        in_specs=[pl.BlockSpec((bsz, d), lambda j: (0, 0)),
                  pl.BlockSpec((d, block_n), lambda j: (0, j)),
                  pl.BlockSpec((1, block_n), lambda j: (0, j))],
        out_specs=pl.BlockSpec((bsz, block_n), lambda j: (0, j)),
        out_shape=jax.ShapeDtypeStruct((bsz, n), F32),
        compiler_params=_params(1),
    )(c, w, b.reshape(1, n))


def _disc_body(lre_ref, lim_ref, ldt_ref, bre_ref, bim_ref, are_ref, aim_ref, obre_ref, obim_ref):
    lre = lre_ref[...]
    lim = lim_ref[...]
    dt = jnp.exp(ldt_ref[...])
    mag = jnp.exp(lre * dt)
    are = mag * jnp.cos(lim * dt)
    aim = mag * jnp.sin(lim * dt)
    are_ref[...] = are
    aim_ref[...] = aim
    nre = are - 1.0
    den = lre * lre + lim * lim
    cre = (nre * lre + aim * lim) / den
    cim = (aim * lre - nre * lim) / den
    bre = bre_ref[...]
    bim = bim_ref[...]
    obre_ref[...] = cre * bre - cim * bim
    obim_ref[...] = cre * bim + cim * bre


def _discretize(lam_re, lam_im, log_dt, b_re_t, b_im_t):
    g, p = lam_re.shape
    h = b_re_t.shape[1]
    are, aim, bbar_re, bbar_im = pl.pallas_call(
        _disc_body,
        out_shape=(jax.ShapeDtypeStruct((g, 1, p), F32), jax.ShapeDtypeStruct((g, 1, p), F32),
                   jax.ShapeDtypeStruct((g, h, p), F32), jax.ShapeDtypeStruct((g, h, p), F32)),
    )(lam_re.reshape(g, 1, p), lam_im.reshape(g, 1, p), log_dt.reshape(g, 1, 1), b_re_t, b_im_t)
    return are.reshape(g, p), aim.reshape(g, p), bbar_re, bbar_im


def _rope_body(pos_ref, f_ref, cos_ref, sin_ref):
    ang = pos_ref[...] * f_ref[...]
    cos_ref[...] = jnp.cos(ang)
    sin_ref[...] = jnp.sin(ang)


def _rope_tables(posf, inv_freq):
    bsz, seq = posf.shape
    nf = inv_freq.shape[0]
    out = jax.ShapeDtypeStruct((nf, bsz, seq), F32)
    return pl.pallas_call(
        _rope_body,
        grid=(nf,),
        in_specs=[pl.BlockSpec((bsz, seq), lambda f: (0, 0)),
                  pl.BlockSpec((None, 1, 1), lambda f: (f, 0, 0))],
        out_specs=(pl.BlockSpec((None, bsz, seq), lambda f: (f, 0, 0)),
                   pl.BlockSpec((None, bsz, seq), lambda f: (f, 0, 0))),
        out_shape=(out, out),
        compiler_params=_params(1),
    )(posf, inv_freq.reshape(nf, 1, 1))


def _inproj_body(x_ref, mod_ref, g1_ref, win_ref, qg_ref, wuq_ref, kvg_ref, wk_ref, wv_ref,
                 t1_ref, t2_ref, u_ref, q_ref, k_ref, v_ref, *, d_model, d_ssm, q_lora, kv_lora, scale):
    x = x_ref[...]
    mod = mod_ref[...]
    shift1 = mod[:, :d_model]
    scale1 = mod[:, d_model:2 * d_model]
    h = _rms(x) * g1_ref[...] * (1.0 + scale1) + shift1
    proj = _dot(h.astype(BF16), win_ref[...])
    u_ref[...] = proj[:, :d_ssm].astype(BF16)

    s2 = d_ssm + q_lora
    s3 = s2 + kv_lora
    qn = (_rms(proj[:, d_ssm:s2]) * qg_ref[...]).astype(BF16)
    q = _dot(qn, wuq_ref[...])
    kvn = (_rms(proj[:, s2:s3]) * kvg_ref[...]).astype(BF16)
    kn = _dot(kvn, wk_ref[...])
    v_t = lax.dot_general(wv_ref[...], kvn, (((1,), (1,)), ((), ())), preferred_element_type=F32)
    row = lax.broadcasted_iota(jnp.int32, (N_HEADS * HEAD_PAD, 1), 0)
    v_ref[...] = (v_t + (row % HEAD_PAD == V_HEAD).astype(F32)).astype(BF16)

    t1 = t1_ref[...]
    t2 = t2_ref[...]
    kr = proj[:, s3:s3 + HEAD_PAD]
    kr = kr * t1 + pltpu.roll(kr, HEAD_PAD - QK_ROPE, 1) * t2
    for hh in range(N_HEADS):
        sl = slice(hh * HEAD_PAD, (hh + 1) * HEAD_PAD)
        qh = q[:, sl]
        qh = qh * t1 + pltpu.roll(qh, HEAD_PAD - QK_ROPE, 1) * t2
        q_ref[:, sl] = (qh * scale).astype(BF16)
        k_ref[:, sl] = (kn[:, sl] + kr).astype(BF16)


def _in_proj(x, mod3, g1, win, qg, wuq, kvg, wk, wv, t1, t2, *, tm, d_ssm, q_lora, kv_lora):
    bsz, seq, d = x.shape
    nt = seq // tm
    hp = N_HEADS * HEAD_PAD
    scale = float((QK_NOPE + QK_ROPE) ** -0.5 * math.log2(math.e))
    const = lambda b, i: (0, 0)
    body = functools.partial(_inproj_body, d_model=d, d_ssm=d_ssm, q_lora=q_lora, kv_lora=kv_lora,
                             scale=scale)
    return pl.pallas_call(
        body,
        grid=(bsz, nt),
        in_specs=[pl.BlockSpec((None, tm, d), lambda b, i: (b, i, 0)),
                  pl.BlockSpec((None, 1, mod3.shape[2]), lambda b, i: (b, 0, 0)),
                  pl.BlockSpec(g1.shape, const),
                  pl.BlockSpec(win.shape, const),
                  pl.BlockSpec(qg.shape, const),
                  pl.BlockSpec(wuq.shape, const),
                  pl.BlockSpec(kvg.shape, const),
                  pl.BlockSpec(wk.shape, const),
                  pl.BlockSpec(wv.shape, const),
                  pl.BlockSpec((None, tm, HEAD_PAD), lambda b, i: (b, i, 0)),
                  pl.BlockSpec((None, tm, HEAD_PAD), lambda b, i: (b, i, 0))],
        out_specs=(pl.BlockSpec((tm, d_ssm), lambda b, i: (i, b)),
                   pl.BlockSpec((None, tm, hp), lambda b, i: (b, i, 0)),
                   pl.BlockSpec((None, tm, hp), lambda b, i: (b, i, 0)),
                   pl.BlockSpec((None, None, hp, tm), lambda b, i: (b, i, 0, 0))),
        out_shape=(jax.ShapeDtypeStruct((seq, bsz * d_ssm), BF16),
                   jax.ShapeDtypeStruct((bsz, seq, hp), BF16),
                   jax.ShapeDtypeStruct((bsz, seq, hp), BF16),
                   jax.ShapeDtypeStruct((bsz, nt, hp, tm), BF16)),
        compiler_params=_params(2),
    )(x, mod3, g1, win, qg, wuq, kvg, wk, wv, t1, t2)


CHAINS_PER_LOOP = 4


def _s5_body(u_ref, bblk_ref, are_ref, aim_ref, cblk_ref, d_ref, wglu_ref, g_ref, y_ref,
             bu_scr, x_scr, st_scr, *, lt, bsz, d_ssm):
    rows = lt * bsz
    n_kt = bblk_ref.shape[0]
    kt_in = bblk_ref.shape[1]
    kt_st = bblk_ref.shape[2]
    pair = 2 * LANE

    @pl.when(pl.program_id(0) == 0)
    def _():
        st_scr[...] = jnp.zeros_like(st_scr)

    u = u_ref[...].reshape(rows, d_ssm)
    for kt in range(n_kt):
        bu_scr[:, kt * kt_st:(kt + 1) * kt_st] = _dot(u[:, kt * kt_in:(kt + 1) * kt_in], bblk_ref[kt])

    n_pairs = (n_kt * kt_st) // pair
    for m0 in range(0, n_pairs, CHAINS_PER_LOOP):
        ms = range(m0, m0 + CHAINS_PER_LOOP)
        ar = [jnp.broadcast_to(are_ref[m:m + 1, :], (bsz, LANE)) for m in ms]
        ai = [jnp.broadcast_to(aim_ref[m:m + 1, :], (bsz, LANE)) for m in ms]
        init = tuple((st_scr[:, m * pair:m * pair + LANE], st_scr[:, m * pair + LANE:(m + 1) * pair])
                     for m in ms)

        def step(t, carry, ms=ms, ar=ar, ai=ai):
            r0 = pl.multiple_of(t * bsz, bsz)
            out = []
            for j, m in enumerate(ms):
                xr, xi = carry[j]
                c0 = m * pair
                br = bu_scr[pl.ds(r0, bsz), c0:c0 + LANE]
                bi = bu_scr[pl.ds(r0, bsz), c0 + LANE:c0 + pair]
                nr = ar[j] * xr - ai[j] * xi + br
                ni = ar[j] * xi + ai[j] * xr + bi
                x_scr[pl.ds(r0, bsz), c0:c0 + LANE] = nr.astype(BF16)
                x_scr[pl.ds(r0, bsz), c0 + LANE:c0 + pair] = ni.astype(BF16)
                out.append((nr, ni))
            return tuple(out)

        fin = lax.fori_loop(0, lt, step, init)
        for j, m in enumerate(ms):
            st_scr[:, m * pair:m * pair + LANE] = fin[j][0]
            st_scr[:, m * pair + LANE:(m + 1) * pair] = fin[j][1]

    ys = [_dot(x_scr[:, kt * kt_st:(kt + 1) * kt_st], cblk_ref[kt]) for kt in range(n_kt)]
    y = jnp.concatenate(ys, axis=1) + d_ref[...] * u.astype(F32)
    y = jax.nn.gelu(y)
    z = _dot(y.astype(BF16), wglu_ref[...])
    o = z[:, :d_ssm] * jax.nn.sigmoid(z[:, d_ssm:])
    o = _rms(o) * g_ref[...]
    y_ref[...] = o.astype(BF16).reshape(lt, bsz, d_ssm)


def _s5(u_tm, bblk, are, aim, cblk, dvec, wglu, g, *, lt):
    seq, bsz, d_ssm = u_tm.shape
    n_state = bblk.shape[0] * bblk.shape[2]
    rows = lt * bsz
    c2 = lambda i: (0, 0)
    c3 = lambda i: (0, 0, 0)
    body = functools.partial(_s5_body, lt=lt, bsz=bsz, d_ssm=d_ssm)
    return pl.pallas_call(
        body,
        grid=(seq // lt,),
        in_specs=[pl.BlockSpec((lt, bsz, d_ssm), lambda i: (i, 0, 0)),
                  pl.BlockSpec(bblk.shape, c3),
                  pl.BlockSpec(are.shape, c2),
                  pl.BlockSpec(aim.shape, c2),
                  pl.BlockSpec(cblk.shape, c3),
                  pl.BlockSpec(dvec.shape, c2),
                  pl.BlockSpec(wglu.shape, c2),
                  pl.BlockSpec(g.shape, c2)],
        out_specs=pl.BlockSpec((lt, bsz, d_ssm), lambda i: (i, 0, 0)),
        out_shape=jax.ShapeDtypeStruct((seq, bsz, d_ssm), BF16),
        scratch_shapes=[pltpu.VMEM((rows, n_state), F32),
                        pltpu.VMEM((rows, n_state), BF16),
                        pltpu.VMEM((bsz, n_state), F32)],
        compiler_params=_params(1),
    )(u_tm, bblk, are, aim, cblk, dvec, wglu, g)


ATTN_HEAD_GROUP = 8


def _attn_body(q_ref, k_ref, vt_ref, g_ref, o_ref, s_scr, m_scr, acc_scr, *, tq, group):
    qi = pl.program_id(1)
    key = lax.broadcasted_iota(jnp.int32, (tq, tq), 0)
    qry = lax.broadcasted_iota(jnp.int32, (tq, tq), 1)
    causal = key <= qry
    contract_last = (((1,), (1,)), ((), ()))

    for h0 in range(0, N_HEADS, group):
        heads = list(range(h0, h0 + group))

        def block(ki, diag, heads=heads):
            k0 = pl.multiple_of(ki * tq, tq)
            for j, hh in enumerate(heads):
                sl = slice(hh * HEAD_PAD, (hh + 1) * HEAD_PAD)
                s = lax.dot_general(k_ref[pl.ds(k0, tq), sl], q_ref[:, sl], contract_last,
                                    preferred_element_type=F32)
                if diag:
                    s = jnp.where(causal, s, MASK_VALUE)
                s_scr[j] = s
            for j, hh in enumerate(heads):
                sl = slice(hh * HEAD_PAD, (hh + 1) * HEAD_PAD)
                s = s_scr[j]
                m_new = jnp.max(s, axis=0, keepdims=True)
                if not diag:
                    m_prev = m_scr[hh]
                    m_new = jnp.maximum(m_prev, m_new)
                p = jnp.exp2(s - m_new).astype(BF16)
                pv = _dot(vt_ref[ki, sl, :], p)
                if diag:
                    acc_scr[hh] = pv
                else:
                    acc_scr[hh] = jnp.exp2(m_prev - m_new) * acc_scr[hh] + pv
                m_scr[hh] = m_new

        block(qi, True)

        def kv_step(ki, carry, block=block):
            block(ki, False)
            return carry

        lax.fori_loop(0, qi, kv_step, 0)

    outs = []
    for hh in range(N_HEADS):
        acc = acc_scr[hh]
        outs.append(acc[:V_HEAD] / acc[V_HEAD:V_HEAD + 1])
    o_t = jnp.concatenate(outs, axis=0)
    o_t = o_t * lax.rsqrt(jnp.mean(o_t * o_t, axis=0, keepdims=True) + EPS)
    o_ref[...] = (o_t.T * g_ref[...]).astype(BF16)


def _attention(q, k, vt, g, *, tq, group):
    bsz, seq, hp = q.shape
    dv = N_HEADS * V_HEAD
    body = functools.partial(_attn_body, tq=tq, group=group)
    return pl.pallas_call(
        body,
        grid=(bsz, seq // tq),
        in_specs=[pl.BlockSpec((None, tq, hp), lambda b, i: (b, i, 0)),
                  pl.BlockSpec((None, seq, hp), lambda b, i: (b, 0, 0)),
                  pl.BlockSpec((None,) + vt.shape[1:], lambda b, i: (b, 0, 0, 0)),
                  pl.BlockSpec(g.shape, lambda b, i: (0, 0))],
        out_specs=pl.BlockSpec((None, tq, dv), lambda b, i: (b, i, 0)),
        out_shape=jax.ShapeDtypeStruct((bsz, seq, dv), BF16),
        scratch_shapes=[pltpu.VMEM((group, tq, tq), F32),
                        pltpu.VMEM((N_HEADS, 1, tq), F32),
                        pltpu.VMEM((N_HEADS, HEAD_PAD, tq), F32)],
        compiler_params=_params(2),
    )(q, k, vt, g)


FF_CHUNK = 1024


def _outffn_body(x_ref, ys_ref, ya_ref, mod_ref, fmod_ref, wos_ref, woa_ref, g2_ref, w1_ref, w2_ref,
                 gf_ref, o_ref, *, d_model):
    d = d_model
    x = x_ref[...]
    mod = mod_ref[...]
    gate1 = mod[:, 2 * d:3 * d]
    shift2 = mod[:, 3 * d:4 * d]
    scale2 = mod[:, 4 * d:5 * d]
    gate2 = mod[:, 5 * d:6 * d]
    mix = _dot(ys_ref[...], wos_ref[...]) + _dot(ya_ref[...], woa_ref[...])
    x1 = x + gate1 * mix
    h = (_rms(x1) * g2_ref[...] * (1.0 + scale2) + shift2).astype(BF16)
    d_ff = w1_ref.shape[1]
    ff = jnp.zeros_like(x1)
    for c0 in range(0, d_ff, FF_CHUNK):
        a = jnp.maximum(_dot(h, w1_ref[:, c0:c0 + FF_CHUNK]), 0.0)
        ff = ff + _dot((a * a).astype(BF16), w2_ref[c0:c0 + FF_CHUNK, :])
    x2 = x1 + gate2 * ff
    fmod = fmod_ref[...]
    fshift = fmod[:, :d]
    fscale = fmod[:, d:2 * d]
    o_ref[...] = _rms(x2) * gf_ref[...] * (1.0 + fscale) + fshift


def _out_ffn(x, ys2d, ya, mod3, fmod3, wos, woa, g2, w1, w2, gf, *, tm):
    bsz, seq, d = x.shape
    d_half = ya.shape[2]
    body = functools.partial(_outffn_body, d_model=d)
    return pl.pallas_call(
        body,
        grid=(bsz, seq // tm),
        in_specs=[pl.BlockSpec((None, tm, d), lambda b, i: (b, i, 0)),
                  pl.BlockSpec((tm, d_half), lambda b, i: (i, b)),
                  pl.BlockSpec((None, tm, d_half), lambda b, i: (b, i, 0)),
                  pl.BlockSpec((None, 1, mod3.shape[2]), lambda b, i: (b, 0, 0)),
                  pl.BlockSpec((None, 1, fmod3.shape[2]), lambda b, i: (b, 0, 0)),
                  _resident(wos.shape),
                  _resident(woa.shape),
                  _resident(g2.shape),
                  _resident(w1.shape),
                  _resident(w2.shape),
                  _resident(gf.shape)],
        out_specs=pl.BlockSpec((None, tm, d), lambda b, i: (b, i, 0)),
        out_shape=jax.ShapeDtypeStruct((bsz, seq, d), F32),
        compiler_params=_params(2),
    )(x, ys2d, ya, mod3, fmod3, wos, woa, g2, w1, w2, gf)


def _pad_heads(w, head_in, pieces):
    k = w.shape[0]
    w3 = w.reshape(k, N_HEADS, head_in)
    cols = []
    for piece in pieces:
        if isinstance(piece, int):
            cols.append(jnp.zeros((k, N_HEADS, piece), w.dtype))
        else:
            cols.append(piece(w3))
    out = jnp.concatenate(cols, axis=2)
    assert out.shape[2] == HEAD_PAD
    return out.reshape(k, N_HEADS * HEAD_PAD)


def _rot_half(wr):
    half = QK_ROPE // 2
    return jnp.concatenate([-wr[..., half:], wr[..., :half]], axis=-1)


def _s5_block_weights(bbar_re, bbar_im, c_re, c_im):
    g, h, p = bbar_re.shape
    n_pair = g // 2
    eye_m = jnp.eye(n_pair, dtype=F32)
    eye_g = jnp.eye(2, dtype=F32)
    bb = jnp.stack([bbar_re, bbar_im], axis=2).reshape(n_pair, 2, h, 2, p)
    b_full = jnp.einsum("mghrp,mn,gk->mghnrkp", bb, eye_m, eye_g).reshape(g * h, g * 2 * p)
    cc = jnp.stack([c_re, -c_im], axis=2).reshape(n_pair, 2, h, 2, p)
    c_full = jnp.einsum("mghrp,mn,gk->nrkpmgh", cc, eye_m, eye_g).reshape(g * 2 * p, g * h)
    n_kt = 2
    kin = (g * h) // n_kt
    kst = (g * 2 * p) // n_kt
    bblk = jnp.stack([b_full[kt * kin:(kt + 1) * kin, kt * kst:(kt + 1) * kst] for kt in range(n_kt)])
    cblk = jnp.stack([c_full[kt * kst:(kt + 1) * kst, kt * kin:(kt + 1) * kin] for kt in range(n_kt)])
    return bblk.astype(BF16), cblk.astype(BF16)


def _forward(x, c, positions, ada_w, ada_b, norm1_g, w_in, ssm_lambda_re, ssm_lambda_im,
             ssm_b_re, ssm_b_im, ssm_c_re, ssm_c_im, ssm_d, ssm_log_dt, w_glu,
             q_norm_g, w_uq, kv_norm_g, w_ukv, ssm_out_g, attn_out_g, w_out,
             norm2_g, w_ff1, w_ff2, final_ada_w, final_ada_b, final_norm_g, *, tm, tq, lt):
    bsz, seq, d = x.shape
    depth = ada_w.shape[0]
    assert depth == 1, "the fused epilogue applies the final norm right after the only layer"
    assert tm == tq, "in_proj emits one V^T tile per attention key block"
    d_ssm = w_glu.shape[1]
    q_lora = w_uq.shape[1]
    kv_lora = w_ukv.shape[1]
    s2 = d_ssm + q_lora
    s3 = s2 + kv_lora

    inv_freq = ROPE_BASE ** (-jnp.arange(0, QK_ROPE, 2, dtype=F32) / QK_ROPE)
    cos_t, sin_t = _rope_tables(positions.astype(F32), inv_freq)
    cos_t = jnp.transpose(cos_t, (1, 2, 0))
    sin_t = jnp.transpose(sin_t, (1, 2, 0))
    ones = jnp.ones((bsz, seq, QK_NOPE), F32)
    zeros_n = jnp.zeros((bsz, seq, QK_NOPE), F32)
    zeros_r = jnp.zeros((bsz, seq, HEAD_PAD - QK_NOPE - QK_ROPE), F32)
    t1 = jnp.concatenate([ones, cos_t, cos_t, zeros_r], axis=-1)
    t2 = jnp.concatenate([zeros_n, sin_t, sin_t, zeros_r], axis=-1)

    fmod = _modulation(c, final_ada_w, final_ada_b)
    fmod3 = fmod.reshape(bsz, 1, fmod.shape[1])

    for l in range(depth):
        mod = _modulation(c, ada_w[l], ada_b[l])
        mod3 = mod.reshape(bsz, 1, mod.shape[1])

        wi = w_in[l]
        wkr = wi[:, s3:]
        win = jnp.concatenate([wi[:, :s3], jnp.zeros((d, QK_NOPE), F32), wkr, _rot_half(wkr)],
                              axis=1).astype(BF16)
        hq = QK_NOPE + QK_ROPE
        wuq = _pad_heads(w_uq[l], hq, [lambda w3: w3[..., :hq],
                                       lambda w3: _rot_half(w3[..., QK_NOPE:])]).astype(BF16)
        wk = _pad_heads(w_ukv[l], QK_NOPE + V_HEAD,
                        [lambda w3: w3[..., :QK_NOPE], HEAD_PAD - QK_NOPE]).astype(BF16)
        wv = _pad_heads(w_ukv[l], QK_NOPE + V_HEAD,
                        [lambda w3: w3[..., QK_NOPE:], HEAD_PAD - V_HEAD]).T.astype(BF16)

        u2d, q, k, v = _in_proj(x, mod3, norm1_g[l].reshape(1, d), win, q_norm_g[l].reshape(1, q_lora),
                                wuq, kv_norm_g[l].reshape(1, kv_lora), wk, wv, t1, t2,
                                tm=tm, d_ssm=d_ssm, q_lora=q_lora, kv_lora=kv_lora)

        are, aim, bbar_re, bbar_im = _discretize(
            ssm_lambda_re[l], ssm_lambda_im[l], ssm_log_dt[l],
            jnp.transpose(ssm_b_re[l], (0, 2, 1)), jnp.transpose(ssm_b_im[l], (0, 2, 1)))
        bblk, cblk = _s5_block_weights(bbar_re, bbar_im, ssm_c_re[l], ssm_c_im[l])
        n_pair = are.shape[0] // 2
        ys_tm = _s5(u2d.reshape(seq, bsz, d_ssm), bblk, are.reshape(n_pair, LANE), aim.reshape(n_pair, LANE),
                    cblk, ssm_d[l].reshape(1, d_ssm), w_glu[l].astype(BF16), ssm_out_g[l].reshape(1, d_ssm),
                    lt=lt)

        ya = _attention(q, k, v, attn_out_g[l].reshape(1, -1), tq=tq, group=ATTN_HEAD_GROUP)

        wo = w_out[l].astype(BF16)
        out = _out_ffn(x, ys_tm.reshape(seq, bsz * d_ssm), ya, mod3, fmod3, wo[:d_ssm], wo[d_ssm:],
                       norm2_g[l].reshape(1, d), w_ff1[l].astype(BF16), w_ff2[l].astype(BF16),
                       final_norm_g.reshape(1, d), tm=tm)
        x = out
    return x


def kernel(x, c, positions, ada_w, ada_b, norm1_g, w_in, ssm_lambda_re, ssm_lambda_im, ssm_b_re, ssm_b_im, ssm_c_re, ssm_c_im, ssm_d, ssm_log_dt, w_glu, q_norm_g, w_uq, kv_norm_g, w_ukv, ssm_out_g, attn_out_g, w_out, norm2_g, w_ff1, w_ff2, final_ada_w, final_ada_b, final_norm_g):
    return _forward(x, c, positions, ada_w, ada_b, norm1_g, w_in, ssm_lambda_re, ssm_lambda_im,
                    ssm_b_re, ssm_b_im, ssm_c_re, ssm_c_im, ssm_d, ssm_log_dt, w_glu,
                    q_norm_g, w_uq, kv_norm_g, w_ukv, ssm_out_g, attn_out_g, w_out,
                    norm2_g, w_ff1, w_ff2, final_ada_w, final_ada_b, final_norm_g,
                    tm=512, tq=512, lt=32)
```

```python
import functools
import math

import jax
import jax.numpy as jnp
from jax import lax
from jax.experimental import pallas as pl
from jax.experimental.pallas import tpu as pltpu

F32 = jnp.float32
BF16 = jnp.bfloat16

SSM_GROUP = 16
SSM_STATE = 64
N_HEADS = 8
QK_NOPE = 64
QK_ROPE = 32
V_HEAD = 64
ROPE_BASE = 10000.0
EPS = 1e-6
LANE = 128
HEAD_PAD = LANE
MASK_VALUE = -1e30
VMEM_LIMIT = 56 * 1024 * 1024


def _rms(x):
    return x * lax.rsqrt(jnp.mean(x * x, axis=-1, keepdims=True) + EPS)


def _dot(a, b):
    return jnp.dot(a, b, preferred_element_type=F32)


def _resident(shape):
    zeros = (0,) * len(shape)
    return pl.BlockSpec(shape, lambda *_: zeros, pipeline_mode=pl.Buffered(1))


def _params(n_grid_dims):
    return pltpu.CompilerParams(dimension_semantics=("arbitrary",) * n_grid_dims,
                                vmem_limit_bytes=VMEM_LIMIT)


def _mod_body(c_ref, w_ref, b_ref, o_ref):
    c = c_ref[...]
    cond = c * jax.nn.sigmoid(c)
    o_ref[...] = jnp.dot(cond, w_ref[...], preferred_element_type=F32,
                         precision=lax.Precision.HIGHEST) + b_ref[...]


def _modulation(c, w, b, block_n=1024):
    bsz, d = c.shape
    n = w.shape[1]
    return pl.pallas_call(
        _mod_body,
        grid=(n // block_n,),
        in_specs=[pl.BlockSpec((bsz, d), lambda j: (0, 0)),
                  pl.BlockSpec((d, block_n), lambda j: (0, j)),
                  pl.BlockSpec((1, block_n), lambda j: (0, j))],
        out_specs=pl.BlockSpec((bsz, block_n), lambda j: (0, j)),
        out_shape=jax.ShapeDtypeStruct((bsz, n), F32),
        compiler_params=_params(1),
    )(c, w, b.reshape(1, n))


def _disc_body(lre_ref, lim_ref, ldt_ref, bre_ref, bim_ref, are_ref, aim_ref, obre_ref, obim_ref):
    lre = lre_ref[...]
    lim = lim_ref[...]
    dt = jnp.exp(ldt_ref[...])
    mag = jnp.exp(lre * dt)
    are = mag * jnp.cos(lim * dt)
    aim = mag * jnp.sin(lim * dt)
    are_ref[...] = are
    aim_ref[...] = aim
    nre = are - 1.0
    den = lre * lre + lim * lim
    cre = (nre * lre + aim * lim) / den
    cim = (aim * lre - nre * lim) / den
    bre = bre_ref[...]
    bim = bim_ref[...]
    obre_ref[...] = cre * bre - cim * bim
    obim_ref[...] = cre * bim + cim * bre


def _discretize(lam_re, lam_im, log_dt, b_re_t, b_im_t):
    g, p = lam_re.shape
    h = b_re_t.shape[1]
    are, aim, bbar_re, bbar_im = pl.pallas_call(
        _disc_body,
        out_shape=(jax.ShapeDtypeStruct((g, 1, p), F32), jax.ShapeDtypeStruct((g, 1, p), F32),
                   jax.ShapeDtypeStruct((g, h, p), F32), jax.ShapeDtypeStruct((g, h, p), F32)),
    )(lam_re.reshape(g, 1, p), lam_im.reshape(g, 1, p), log_dt.reshape(g, 1, 1), b_re_t, b_im_t)
    return are.reshape(g, p), aim.reshape(g, p), bbar_re, bbar_im


def _rope_body(pos_ref, f_ref, t1_ref, t2_ref):
    ang = f_ref[...] * pos_ref[...]
    cos = jnp.cos(ang)
    sin = jnp.sin(ang)
    half = QK_ROPE // 2
    seq = ang.shape[1]
    t1_ref[:QK_NOPE, :] = jnp.ones((QK_NOPE, seq), F32)
    t2_ref[:QK_NOPE, :] = jnp.zeros((QK_NOPE, seq), F32)
    for r0 in (QK_NOPE, QK_NOPE + half):
        t1_ref[r0:r0 + half, :] = cos
        t2_ref[r0:r0 + half, :] = sin
    pad = HEAD_PAD - QK_NOPE - QK_ROPE
    t1_ref[HEAD_PAD - pad:, :] = jnp.zeros((pad, seq), F32)
    t2_ref[HEAD_PAD - pad:, :] = jnp.zeros((pad, seq), F32)


def _rope_tables(posf, inv_freq):
    bsz, seq = posf.shape
    nf = inv_freq.shape[0]
    out = jax.ShapeDtypeStruct((bsz, HEAD_PAD, seq), F32)
    return pl.pallas_call(
        _rope_body,
        grid=(bsz,),
        in_specs=[pl.BlockSpec((None, 1, seq), lambda b: (b, 0, 0)),
                  pl.BlockSpec((nf, 1), lambda b: (0, 0))],
        out_specs=(pl.BlockSpec((None, HEAD_PAD, seq), lambda b: (b, 0, 0)),
                   pl.BlockSpec((None, HEAD_PAD, seq), lambda b: (b, 0, 0))),
        out_shape=(out, out),
        compiler_params=_params(1),
    )(posf.reshape(bsz, 1, seq), inv_freq.reshape(nf, 1))


def _inproj_body(x_ref, mod_ref, g1_ref, win_ref, qg_ref, wuq_ref, kvg_ref, wk_ref, wv_ref,
                 t1_ref, t2_ref, u_ref, q_ref, k_ref, v_ref, *, d_model, d_ssm, q_lora, kv_lora, scale):
    x = x_ref[...]
    mod = mod_ref[...]
    shift1 = mod[:, :d_model]
    scale1 = mod[:, d_model:2 * d_model]
    h = _rms(x) * g1_ref[...] * (1.0 + scale1) + shift1
    proj = _dot(h.astype(BF16), win_ref[...])
    u_ref[...] = proj[:, :d_ssm].astype(BF16)

    s2 = d_ssm + q_lora
    s3 = s2 + kv_lora
    qn = (_rms(proj[:, d_ssm:s2]) * qg_ref[...]).astype(BF16)
    q = _dot(qn, wuq_ref[...])
    kvn = (_rms(proj[:, s2:s3]) * kvg_ref[...]).astype(BF16)
    kn = _dot(kvn, wk_ref[...])
    v_t = lax.dot_general(wv_ref[...], kvn, (((1,), (1,)), ((), ())), preferred_element_type=F32)
    row = lax.broadcasted_iota(jnp.int32, (N_HEADS * HEAD_PAD, 1), 0)
    v_ref[...] = (v_t + (row % HEAD_PAD == V_HEAD).astype(F32)).astype(BF16)

    t1 = t1_ref[...].T
    t2 = t2_ref[...].T
    kr = proj[:, s3:s3 + HEAD_PAD]
    kr = kr * t1 + pltpu.roll(kr, HEAD_PAD - QK_ROPE, 1) * t2
    for hh in range(N_HEADS):
        sl = slice(hh * HEAD_PAD, (hh + 1) * HEAD_PAD)
        qh = q[:, sl]
        qh = qh * t1 + pltpu.roll(qh, HEAD_PAD - QK_ROPE, 1) * t2
        q_ref[:, sl] = (qh * scale).astype(BF16)
        k_ref[:, sl] = (kn[:, sl] + kr).astype(BF16)


def _in_proj(x, mod3, g1, win, qg, wuq, kvg, wk, wv, t1, t2, *, tm, d_ssm, q_lora, kv_lora):
    bsz, seq, d = x.shape
    nt = seq // tm
    hp = N_HEADS * HEAD_PAD
    scale = float((QK_NOPE + QK_ROPE) ** -0.5 * math.log2(math.e))
    const = lambda b, i: (0, 0)
    body = functools.partial(_inproj_body, d_model=d, d_ssm=d_ssm, q_lora=q_lora, kv_lora=kv_lora,
                             scale=scale)
    return pl.pallas_call(
        body,
        grid=(bsz, nt),
        in_specs=[pl.BlockSpec((None, tm, d), lambda b, i: (b, i, 0)),
                  pl.BlockSpec((None, 1, mod3.shape[2]), lambda b, i: (b, 0, 0)),
                  pl.BlockSpec(g1.shape, const),
                  pl.BlockSpec(win.shape, const),
                  pl.BlockSpec(qg.shape, const),
                  pl.BlockSpec(wuq.shape, const),
                  pl.BlockSpec(kvg.shape, const),
                  pl.BlockSpec(wk.shape, const),
                  pl.BlockSpec(wv.shape, const),
                  pl.BlockSpec((None, HEAD_PAD, tm), lambda b, i: (b, 0, i)),
                  pl.BlockSpec((None, HEAD_PAD, tm), lambda b, i: (b, 0, i))],
        out_specs=(pl.BlockSpec((tm, d_ssm), lambda b, i: (i, b)),
                   pl.BlockSpec((None, tm, hp), lambda b, i: (b, i, 0)),
                   pl.BlockSpec((None, tm, hp), lambda b, i: (b, i, 0)),
                   pl.BlockSpec((None, None, hp, tm), lambda b, i: (b, i, 0, 0))),
        out_shape=(jax.ShapeDtypeStruct((seq, bsz * d_ssm), BF16),
                   jax.ShapeDtypeStruct((bsz, seq, hp), BF16),
                   jax.ShapeDtypeStruct((bsz, seq, hp), BF16),
                   jax.ShapeDtypeStruct((bsz, nt, hp, tm), BF16)),
        compiler_params=_params(2),
    )(x, mod3, g1, win, qg, wuq, kvg, wk, wv, t1, t2)


S5_SPLITS = 4


def _s5_body(u_ref, bblk_ref, are_ref, aim_ref, cblk_ref, d_ref, wglu_ref, g_ref, y_ref,
             bu_scr, x_scr, st_scr, *, lt, bsz, d_ssm):
    rows = lt * bsz
    n_split = bblk_ref.shape[0]
    sp_in = bblk_ref.shape[1]
    sp_st = bblk_ref.shape[2]
    pair = 2 * LANE
    pairs_per_split = sp_st // pair

    @pl.when(pl.program_id(0) == 0)
    def _():
        st_scr[...] = jnp.zeros_like(st_scr)

    u = u_ref[...].reshape(rows, d_ssm)
    ys = []
    for sp in range(n_split):
        bu_scr[:, sp * sp_st:(sp + 1) * sp_st] = _dot(u[:, sp * sp_in:(sp + 1) * sp_in], bblk_ref[sp])

        ms = range(sp * pairs_per_split, (sp + 1) * pairs_per_split)
        ar = [jnp.broadcast_to(are_ref[m:m + 1, :], (bsz, LANE)) for m in ms]
        ai = [jnp.broadcast_to(aim_ref[m:m + 1, :], (bsz, LANE)) for m in ms]
        state = [(st_scr[:, m * pair:m * pair + LANE], st_scr[:, m * pair + LANE:(m + 1) * pair])
                 for m in ms]
        for t in range(lt):
            r0 = t * bsz
            for j, m in enumerate(ms):
                xr, xi = state[j]
                c0 = m * pair
                nr = ar[j] * xr - ai[j] * xi + bu_scr[r0:r0 + bsz, c0:c0 + LANE]
                ni = ar[j] * xi + ai[j] * xr + bu_scr[r0:r0 + bsz, c0 + LANE:c0 + pair]
                x_scr[r0:r0 + bsz, c0:c0 + LANE] = nr.astype(BF16)
                x_scr[r0:r0 + bsz, c0 + LANE:c0 + pair] = ni.astype(BF16)
                state[j] = (nr, ni)
        for j, m in enumerate(ms):
            st_scr[:, m * pair:m * pair + LANE] = state[j][0]
            st_scr[:, m * pair + LANE:(m + 1) * pair] = state[j][1]

        ys.append(_dot(x_scr[:, sp * sp_st:(sp + 1) * sp_st], cblk_ref[sp]))
    y = jnp.concatenate(ys, axis=1) + d_ref[...] * u.astype(F32)
    y = jax.nn.gelu(y)
    z = _dot(y.astype(BF16), wglu_ref[...])
    o = z[:, :d_ssm] * jax.nn.sigmoid(z[:, d_ssm:])
    o = _rms(o) * g_ref[...]
    y_ref[...] = o.astype(BF16).reshape(lt, bsz, d_ssm)


def _s5(u_tm, bblk, are, aim, cblk, dvec, wglu, g, *, lt):
    seq, bsz, d_ssm = u_tm.shape
    n_state = bblk.shape[0] * bblk.shape[2]
    rows = lt * bsz
    c2 = lambda i: (0, 0)
    c3 = lambda i: (0, 0, 0)
    body = functools.partial(_s5_body, lt=lt, bsz=bsz, d_ssm=d_ssm)
    return pl.pallas_call(
        body,
        grid=(seq // lt,),
        in_specs=[pl.BlockSpec((lt, bsz, d_ssm), lambda i: (i, 0, 0)),
                  pl.BlockSpec(bblk.shape, c3),
                  pl.BlockSpec(are.shape, c2),
                  pl.BlockSpec(aim.shape, c2),
                  pl.BlockSpec(cblk.shape, c3),
                  pl.BlockSpec(dvec.shape, c2),
                  pl.BlockSpec(wglu.shape, c2),
                  pl.BlockSpec(g.shape, c2)],
        out_specs=pl.BlockSpec((lt, bsz, d_ssm), lambda i: (i, 0, 0)),
        out_shape=jax.ShapeDtypeStruct((seq, bsz, d_ssm), BF16),
        scratch_shapes=[pltpu.VMEM((rows, n_state), F32),
                        pltpu.VMEM((rows, n_state), BF16),
                        pltpu.VMEM((bsz, n_state), F32)],
        compiler_params=_params(1),
    )(u_tm, bblk, are, aim, cblk, dvec, wglu, g)


ATTN_HEAD_GROUP = 8


def _attn_body(q_ref, k_ref, vt_ref, g_ref, o_ref, s_scr, m_scr, acc_scr, *, tq, group):
    qi = pl.program_id(1)
    half = tq // 2
    contract_last = (((1,), (1,)), ((), ()))
    diag_parts = ((half, 0), (tq, half))

    for h0 in range(0, N_HEADS, group):
        heads = list(range(h0, h0 + group))

        def diag_block(heads=heads):
            k0 = pl.multiple_of(qi * tq, tq)
            for j, hh in enumerate(heads):
                sl = slice(hh * HEAD_PAD, (hh + 1) * HEAD_PAD)
                for nk, q0 in diag_parts:
                    s = lax.dot_general(k_ref[pl.ds(k0, nk), sl], q_ref[q0:q0 + half, sl], contract_last,
                                        preferred_element_type=F32)
                    key = lax.broadcasted_iota(jnp.int32, (nk, half), 0)
                    qry = lax.broadcasted_iota(jnp.int32, (nk, half), 1) + q0
                    s_scr[j, :nk, q0:q0 + half] = jnp.where(key <= qry, s, MASK_VALUE)
            for j, hh in enumerate(heads):
                sl = slice(hh * HEAD_PAD, (hh + 1) * HEAD_PAD)
                for nk, q0 in diag_parts:
                    s = s_scr[j, :nk, q0:q0 + half]
                    m_new = jnp.max(s, axis=0, keepdims=True)
                    p = jnp.exp2(s - m_new).astype(BF16)
                    acc_scr[hh, :, q0:q0 + half] = _dot(vt_ref[qi, sl, :nk], p)
                    m_scr[hh, :, q0:q0 + half] = m_new

        def full_block(ki, heads=heads):
            k0 = pl.multiple_of(ki * tq, tq)
            for j, hh in enumerate(heads):
                sl = slice(hh * HEAD_PAD, (hh + 1) * HEAD_PAD)
                s_scr[j] = lax.dot_general(k_ref[pl.ds(k0, tq), sl], q_ref[:, sl], contract_last,
                                           preferred_element_type=F32)
            for j, hh in enumerate(heads):
                sl = slice(hh * HEAD_PAD, (hh + 1) * HEAD_PAD)
                s = s_scr[j]
                m_prev = m_scr[hh]
                m_new = jnp.maximum(m_prev, jnp.max(s, axis=0, keepdims=True))
                p = jnp.exp2(s - m_new).astype(BF16)
                pv = _dot(vt_ref[ki, sl, :], p)
                acc_scr[hh] = jnp.exp2(m_prev - m_new) * acc_scr[hh] + pv
                m_scr[hh] = m_new

        diag_block()

        def kv_step(ki, carry, full_block=full_block):
            full_block(ki)
            return carry

        lax.fori_loop(0, qi, kv_step, 0)

    outs = []
    for hh in range(N_HEADS):
        acc = acc_scr[hh]
        outs.append(acc[:V_HEAD] / acc[V_HEAD:V_HEAD + 1])
    o_t = jnp.concatenate(outs, axis=0)
    o_t = o_t * lax.rsqrt(jnp.mean(o_t * o_t, axis=0, keepdims=True) + EPS)
    o_ref[...] = (o_t.T * g_ref[...]).astype(BF16)


def _attention(q, k, vt, g, *, tq, group):
    bsz, seq, hp = q.shape
    dv = N_HEADS * V_HEAD
    body = functools.partial(_attn_body, tq=tq, group=group)
    return pl.pallas_call(
        body,
        grid=(bsz, seq // tq),
        in_specs=[pl.BlockSpec((None, tq, hp), lambda b, i: (b, i, 0)),
                  pl.BlockSpec((None, seq, hp), lambda b, i: (b, 0, 0)),
                  pl.BlockSpec((None,) + vt.shape[1:], lambda b, i: (b, 0, 0, 0)),
                  pl.BlockSpec(g.shape, lambda b, i: (0, 0))],
        out_specs=pl.BlockSpec((None, tq, dv), lambda b, i: (b, i, 0)),
        out_shape=jax.ShapeDtypeStruct((bsz, seq, dv), BF16),
        scratch_shapes=[pltpu.VMEM((group, tq, tq), F32),
                        pltpu.VMEM((N_HEADS, 1, tq), F32),
                        pltpu.VMEM((N_HEADS, HEAD_PAD, tq), F32)],
        compiler_params=_params(2),
    )(q, k, vt, g)


FF_CHUNK = 1024


def _outffn_body(x_ref, ys_ref, ya_ref, mod_ref, fmod_ref, wos_ref, woa_ref, g2_ref, w1_ref, w2_ref,
                 gf_ref, o_ref, *, d_model):
    d = d_model
    x = x_ref[...]
    mod = mod_ref[...]
    gate1 = mod[:, 2 * d:3 * d]
    shift2 = mod[:, 3 * d:4 * d]
    scale2 = mod[:, 4 * d:5 * d]
    gate2 = mod[:, 5 * d:6 * d]
    mix = _dot(ys_ref[...], wos_ref[...]) + _dot(ya_ref[...], woa_ref[...])
    x1 = x + gate1 * mix
    h = (_rms(x1) * g2_ref[...] * (1.0 + scale2) + shift2).astype(BF16)
    d_ff = w1_ref.shape[1]
    ff = jnp.zeros_like(x1)
    for c0 in range(0, d_ff, FF_CHUNK):
        a = jnp.maximum(_dot(h, w1_ref[:, c0:c0 + FF_CHUNK]), 0.0)
        ff = ff + _dot((a * a).astype(BF16), w2_ref[c0:c0 + FF_CHUNK, :])
    x2 = x1 + gate2 * ff
    fmod = fmod_ref[...]
    fshift = fmod[:, :d]
    fscale = fmod[:, d:2 * d]
    o_ref[...] = _rms(x2) * gf_ref[...] * (1.0 + fscale) + fshift


def _out_ffn(x, ys2d, ya, mod3, fmod3, wos, woa, g2, w1, w2, gf, *, tm):
    bsz, seq, d = x.shape
    d_half = ya.shape[2]
    body = functools.partial(_outffn_body, d_model=d)
    return pl.pallas_call(
        body,
        grid=(bsz, seq // tm),
        in_specs=[pl.BlockSpec((None, tm, d), lambda b, i: (b, i, 0)),
                  pl.BlockSpec((tm, d_half), lambda b, i: (i, b)),
                  pl.BlockSpec((None, tm, d_half), lambda b, i: (b, i, 0)),
                  pl.BlockSpec((None, 1, mod3.shape[2]), lambda b, i: (b, 0, 0)),
                  pl.BlockSpec((None, 1, fmod3.shape[2]), lambda b, i: (b, 0, 0)),
                  _resident(wos.shape),
                  _resident(woa.shape),
                  _resident(g2.shape),
                  _resident(w1.shape),
                  _resident(w2.shape),
                  _resident(gf.shape)],
        out_specs=pl.BlockSpec((None, tm, d), lambda b, i: (b, i, 0)),
        out_shape=jax.ShapeDtypeStruct((bsz, seq, d), F32),
        compiler_params=_params(2),
    )(x, ys2d, ya, mod3, fmod3, wos, woa, g2, w1, w2, gf)


def _pad_heads(w, head_in, pieces):
    k = w.shape[0]
    w3 = w.reshape(k, N_HEADS, head_in)
    cols = []
    for piece in pieces:
        if isinstance(piece, int):
            cols.append(jnp.zeros((k, N_HEADS, piece), w.dtype))
        else:
            cols.append(piece(w3))
    out = jnp.concatenate(cols, axis=2)
    assert out.shape[2] == HEAD_PAD
    return out.reshape(k, N_HEADS * HEAD_PAD)


def _rot_half(wr):
    half = QK_ROPE // 2
    return jnp.concatenate([-wr[..., half:], wr[..., :half]], axis=-1)


def _s5_block_weights(bbar_re, bbar_im, c_re, c_im):
    g, h, p = bbar_re.shape
    n_pair = g // 2
    eye_m = jnp.eye(n_pair, dtype=F32)
    eye_g = jnp.eye(2, dtype=F32)
    bb = jnp.stack([bbar_re, bbar_im], axis=2).reshape(n_pair, 2, h, 2, p)
    b_full = jnp.einsum("mghrp,mn,gk->mghnrkp", bb, eye_m, eye_g).reshape(g * h, g * 2 * p)
    cc = jnp.stack([c_re, -c_im], axis=2).reshape(n_pair, 2, h, 2, p)
    c_full = jnp.einsum("mghrp,mn,gk->nrkpmgh", cc, eye_m, eye_g).reshape(g * 2 * p, g * h)
    n_kt = S5_SPLITS
    kin = (g * h) // n_kt
    kst = (g * 2 * p) // n_kt
    bblk = jnp.stack([b_full[kt * kin:(kt + 1) * kin, kt * kst:(kt + 1) * kst] for kt in range(n_kt)])
    cblk = jnp.stack([c_full[kt * kst:(kt + 1) * kst, kt * kin:(kt + 1) * kin] for kt in range(n_kt)])
    return bblk.astype(BF16), cblk.astype(BF16)


def _forward(x, c, positions, ada_w, ada_b, norm1_g, w_in, ssm_lambda_re, ssm_lambda_im,
             ssm_b_re, ssm_b_im, ssm_c_re, ssm_c_im, ssm_d, ssm_log_dt, w_glu,
             q_norm_g, w_uq, kv_norm_g, w_ukv, ssm_out_g, attn_out_g, w_out,
             norm2_g, w_ff1, w_ff2, final_ada_w, final_ada_b, final_norm_g, *, tm, tq, lt):
    bsz, seq, d = x.shape
    depth = ada_w.shape[0]
    assert depth == 1, "the fused epilogue applies the final norm right after the only layer"
    assert tm == tq, "in_proj emits one V^T tile per attention key block"
    d_ssm = w_glu.shape[1]
    q_lora = w_uq.shape[1]
    kv_lora = w_ukv.shape[1]
    s2 = d_ssm + q_lora
    s3 = s2 + kv_lora

    inv_freq = ROPE_BASE ** (-jnp.arange(0, QK_ROPE, 2, dtype=F32) / QK_ROPE)
    t1, t2 = _rope_tables(positions.astype(F32), inv_freq)

    fmod = _modulation(c, final_ada_w, final_ada_b)
    fmod3 = fmod.reshape(bsz, 1, fmod.shape[1])

    for l in range(depth):
        mod = _modulation(c, ada_w[l], ada_b[l])
        mod3 = mod.reshape(bsz, 1, mod.shape[1])

        wi = w_in[l]
        wkr = wi[:, s3:]
        win = jnp.concatenate([wi[:, :s3], jnp.zeros((d, QK_NOPE), F32), wkr, _rot_half(wkr)],
                              axis=1).astype(BF16)
        hq = QK_NOPE + QK_ROPE
        wuq = _pad_heads(w_uq[l], hq, [lambda w3: w3[..., :hq],
                                       lambda w3: _rot_half(w3[..., QK_NOPE:])]).astype(BF16)
        wk = _pad_heads(w_ukv[l], QK_NOPE + V_HEAD,
                        [lambda w3: w3[..., :QK_NOPE], HEAD_PAD - QK_NOPE]).astype(BF16)
        wv = _pad_heads(w_ukv[l], QK_NOPE + V_HEAD,
                        [lambda w3: w3[..., QK_NOPE:], HEAD_PAD - V_HEAD]).T.astype(BF16)

        u2d, q, k, v = _in_proj(x, mod3, norm1_g[l].reshape(1, d), win, q_norm_g[l].reshape(1, q_lora),
                                wuq, kv_norm_g[l].reshape(1, kv_lora), wk, wv, t1, t2,
                                tm=tm, d_ssm=d_ssm, q_lora=q_lora, kv_lora=kv_lora)

        are, aim, bbar_re, bbar_im = _discretize(
            ssm_lambda_re[l], ssm_lambda_im[l], ssm_log_dt[l],
            jnp.transpose(ssm_b_re[l], (0, 2, 1)), jnp.transpose(ssm_b_im[l], (0, 2, 1)))
        bblk, cblk = _s5_block_weights(bbar_re, bbar_im, ssm_c_re[l], ssm_c_im[l])
        n_pair = are.shape[0] // 2
        ys_tm = _s5(u2d.reshape(seq, bsz, d_ssm), bblk, are.reshape(n_pair, LANE), aim.reshape(n_pair, LANE),
                    cblk, ssm_d[l].reshape(1, d_ssm), w_glu[l].astype(BF16), ssm_out_g[l].reshape(1, d_ssm),
                    lt=lt)

        ya = _attention(q, k, v, attn_out_g[l].reshape(1, -1), tq=tq, group=ATTN_HEAD_GROUP)

        wo = w_out[l].astype(BF16)
        out = _out_ffn(x, ys_tm.reshape(seq, bsz * d_ssm), ya, mod3, fmod3, wo[:d_ssm], wo[d_ssm:],
                       norm2_g[l].reshape(1, d), w_ff1[l].astype(BF16), w_ff2[l].astype(BF16),
                       final_norm_g.reshape(1, d), tm=tm)
        x = out
    return x


def kernel(x, c, positions, ada_w, ada_b, norm1_g, w_in, ssm_lambda_re, ssm_lambda_im, ssm_b_re, ssm_b_im, ssm_c_re, ssm_c_im, ssm_d, ssm_log_dt, w_glu, q_norm_g, w_uq, kv_norm_g, w_ukv, ssm_out_g, attn_out_g, w_out, norm2_g, w_ff1, w_ff2, final_ada_w, final_ada_b, final_norm_g):
    return _forward(x, c, positions, ada_w, ada_b, norm1_g, w_in, ssm_lambda_re, ssm_lambda_im,
                    ssm_b_re, ssm_b_im, ssm_c_re, ssm_c_im, ssm_d, ssm_log_dt, w_glu,
                    q_norm_g, w_uq, kv_norm_g, w_ukv, ssm_out_g, attn_out_g, w_out,
                    norm2_g, w_ff1, w_ff2, final_ada_w, final_ada_b, final_norm_g,
                    tm=512, tq=512, lt=32)
```

```python
import functools
import math

import jax
import jax.numpy as jnp
from jax import lax
from jax.experimental import pallas as pl
from jax.experimental.pallas import tpu as pltpu

F32 = jnp.float32
BF16 = jnp.bfloat16

SSM_GROUP = 16
SSM_STATE = 64
N_HEADS = 8
QK_NOPE = 64
QK_ROPE = 32
V_HEAD = 64
ROPE_BASE = 10000.0
EPS = 1e-6
LANE = 128
HEAD_PAD = LANE
V_ROWS = V_HEAD + 16
IN_PROJ_PARTS = 2
MASK_VALUE = -1e30
VMEM_LIMIT = 56 * 1024 * 1024


def _rms(x):
    return x * lax.rsqrt(jnp.mean(x * x, axis=-1, keepdims=True) + EPS)


def _dot(a, b):
    return jnp.dot(a, b, preferred_element_type=F32)


def _resident(shape):
    zeros = (0,) * len(shape)
    return pl.BlockSpec(shape, lambda *_: zeros, pipeline_mode=pl.Buffered(1))


def _params(n_grid_dims):
    return pltpu.CompilerParams(dimension_semantics=("arbitrary",) * n_grid_dims,
                                vmem_limit_bytes=VMEM_LIMIT)


def _mod_body(c_ref, w_ref, b_ref, o_ref):
    c = c_ref[...]
    cond = c * jax.nn.sigmoid(c)
    o_ref[...] = jnp.dot(cond, w_ref[...], preferred_element_type=F32,
                         precision=lax.Precision.HIGHEST) + b_ref[...]


def _modulation(c, w, b, block_n=1024):
    bsz, d = c.shape
    n = w.shape[1]
    return pl.pallas_call(
        _mod_body,
        grid=(n // block_n,),
        in_specs=[pl.BlockSpec((bsz, d), lambda j: (0, 0)),
                  pl.BlockSpec((d, block_n), lambda j: (0, j)),
                  pl.BlockSpec((1, block_n), lambda j: (0, j))],
        out_specs=pl.BlockSpec((bsz, block_n), lambda j: (0, j)),
        out_shape=jax.ShapeDtypeStruct((bsz, n), F32),
        compiler_params=_params(1),
    )(c, w, b.reshape(1, n))


def _disc_body(lre_ref, lim_ref, ldt_ref, bre_ref, bim_ref, are_ref, aim_ref, obre_ref, obim_ref):
    lre = lre_ref[...]
    lim = lim_ref[...]
    dt = jnp.exp(ldt_ref[...])
    mag = jnp.exp(lre * dt)
    are = mag * jnp.cos(lim * dt)
    aim = mag * jnp.sin(lim * dt)
    are_ref[...] = are
    aim_ref[...] = aim
    nre = are - 1.0
    den = lre * lre + lim * lim
    cre = (nre * lre + aim * lim) / den
    cim = (aim * lre - nre * lim) / den
    bre = bre_ref[...]
    bim = bim_ref[...]
    obre_ref[...] = cre * bre - cim * bim
    obim_ref[...] = cre * bim + cim * bre


def _discretize(lam_re, lam_im, log_dt, b_re_t, b_im_t):
    g, p = lam_re.shape
    h = b_re_t.shape[1]
    are, aim, bbar_re, bbar_im = pl.pallas_call(
        _disc_body,
        out_shape=(jax.ShapeDtypeStruct((g, 1, p), F32), jax.ShapeDtypeStruct((g, 1, p), F32),
                   jax.ShapeDtypeStruct((g, h, p), F32), jax.ShapeDtypeStruct((g, h, p), F32)),
    )(lam_re.reshape(g, 1, p), lam_im.reshape(g, 1, p), log_dt.reshape(g, 1, 1), b_re_t, b_im_t)
    return are.reshape(g, p), aim.reshape(g, p), bbar_re, bbar_im


def _rope_body(pos_ref, f_ref, t1_ref, t2_ref):
    ang = f_ref[...] * pos_ref[...]
    cos = jnp.cos(ang)
    sin = jnp.sin(ang)
    half = QK_ROPE // 2
    seq = ang.shape[1]
    t1_ref[:QK_NOPE, :] = jnp.ones((QK_NOPE, seq), F32)
    t2_ref[:QK_NOPE, :] = jnp.zeros((QK_NOPE, seq), F32)
    for r0 in (QK_NOPE, QK_NOPE + half):
        t1_ref[r0:r0 + half, :] = cos
        t2_ref[r0:r0 + half, :] = sin
    pad = HEAD_PAD - QK_NOPE - QK_ROPE
    t1_ref[HEAD_PAD - pad:, :] = jnp.zeros((pad, seq), F32)
    t2_ref[HEAD_PAD - pad:, :] = jnp.zeros((pad, seq), F32)


def _rope_tables(posf, inv_freq):
    bsz, seq = posf.shape
    nf = inv_freq.shape[0]
    out = jax.ShapeDtypeStruct((bsz, HEAD_PAD, seq), F32)
    return pl.pallas_call(
        _rope_body,
        grid=(bsz,),
        in_specs=[pl.BlockSpec((None, 1, seq), lambda b: (b, 0, 0)),
                  pl.BlockSpec((nf, 1), lambda b: (0, 0))],
        out_specs=(pl.BlockSpec((None, HEAD_PAD, seq), lambda b: (b, 0, 0)),
                   pl.BlockSpec((None, HEAD_PAD, seq), lambda b: (b, 0, 0))),
        out_shape=(out, out),
        compiler_params=_params(1),
    )(posf.reshape(bsz, 1, seq), inv_freq.reshape(nf, 1))


def _inproj_body(x_ref, mod_ref, g1_ref, win_ref, qg_ref, wuq_ref, kvg_ref, wk_ref, wv_ref,
                 t1_ref, t2_ref, u_ref, q_ref, k_ref, v_ref, *, d_model, d_ssm, q_lora, kv_lora, scale):
    mod = mod_ref[...]
    shift1 = mod[:, :d_model]
    scale1 = mod[:, d_model:2 * d_model]
    s2 = d_ssm + q_lora
    s3 = s2 + kv_lora
    tm = x_ref.shape[0]
    part = tm // IN_PROJ_PARTS
    row = lax.broadcasted_iota(jnp.int32, (v_ref.shape[0], 1), 0)
    ones_rows = (row % V_ROWS == V_HEAD).astype(F32)
    for r0 in range(0, tm, part):
        rows = slice(r0, r0 + part)
        h = _rms(x_ref[rows, :]) * g1_ref[...] * (1.0 + scale1) + shift1
        proj = _dot(h.astype(BF16), win_ref[...])
        u_ref[rows, :] = proj[:, :d_ssm].astype(BF16)

        qn = (_rms(proj[:, d_ssm:s2]) * qg_ref[...]).astype(BF16)
        q = _dot(qn, wuq_ref[...])
        kvn = (_rms(proj[:, s2:s3]) * kvg_ref[...]).astype(BF16)
        kn = _dot(kvn, wk_ref[...])
        v_t = lax.dot_general(wv_ref[...], kvn, (((1,), (1,)), ((), ())), preferred_element_type=F32)
        v_ref[:, rows] = (v_t + ones_rows).astype(BF16)

        t1 = t1_ref[:, rows].T
        t2 = t2_ref[:, rows].T
        kr = proj[:, s3:s3 + HEAD_PAD]
        kr = kr * t1 + pltpu.roll(kr, HEAD_PAD - QK_ROPE, 1) * t2
        for hh in range(N_HEADS):
            sl = slice(hh * HEAD_PAD, (hh + 1) * HEAD_PAD)
            qh = q[:, sl]
            qh = qh * t1 + pltpu.roll(qh, HEAD_PAD - QK_ROPE, 1) * t2
            q_ref[rows, sl] = (qh * scale).astype(BF16)
            k_ref[rows, sl] = (kn[:, sl] + kr).astype(BF16)


def _in_proj(x, mod3, g1, win, qg, wuq, kvg, wk, wv, t1, t2, *, tm, d_ssm, q_lora, kv_lora):
    bsz, seq, d = x.shape
    nt = seq // tm
    hp = N_HEADS * HEAD_PAD
    scale = float((QK_NOPE + QK_ROPE) ** -0.5 * math.log2(math.e))
    const = lambda b, i: (0, 0)
    body = functools.partial(_inproj_body, d_model=d, d_ssm=d_ssm, q_lora=q_lora, kv_lora=kv_lora,
                             scale=scale)
    return pl.pallas_call(
        body,
        grid=(bsz, nt),
        in_specs=[pl.BlockSpec((None, tm, d), lambda b, i: (b, i, 0)),
                  pl.BlockSpec((None, 1, mod3.shape[2]), lambda b, i: (b, 0, 0)),
                  pl.BlockSpec(g1.shape, const),
                  pl.BlockSpec(win.shape, const),
                  pl.BlockSpec(qg.shape, const),
                  pl.BlockSpec(wuq.shape, const),
                  pl.BlockSpec(kvg.shape, const),
                  pl.BlockSpec(wk.shape, const),
                  pl.BlockSpec(wv.shape, const),
                  pl.BlockSpec((None, HEAD_PAD, tm), lambda b, i: (b, 0, i)),
                  pl.BlockSpec((None, HEAD_PAD, tm), lambda b, i: (b, 0, i))],
        out_specs=(pl.BlockSpec((tm, d_ssm), lambda b, i: (i, b)),
                   pl.BlockSpec((None, tm, hp), lambda b, i: (b, i, 0)),
                   pl.BlockSpec((None, tm, hp), lambda b, i: (b, i, 0)),
                   pl.BlockSpec((None, None, wv.shape[0], tm), lambda b, i: (b, i, 0, 0))),
        out_shape=(jax.ShapeDtypeStruct((seq, bsz * d_ssm), BF16),
                   jax.ShapeDtypeStruct((bsz, seq, hp), BF16),
                   jax.ShapeDtypeStruct((bsz, seq, hp), BF16),
                   jax.ShapeDtypeStruct((bsz, nt, wv.shape[0], tm), BF16)),
        compiler_params=_params(2),
    )(x, mod3, g1, win, qg, wuq, kvg, wk, wv, t1, t2)


S5_SPLITS = 4


def _s5_body(u_ref, bblk_ref, are_ref, aim_ref, cblk_ref, d_ref, wglu_ref, g_ref, y_ref,
             bu_scr, x_scr, st_scr, *, lt, bsz, d_ssm):
    rows = lt * bsz
    n_split = bblk_ref.shape[0]
    sp_in = bblk_ref.shape[1]
    sp_st = bblk_ref.shape[2]
    pair = 2 * LANE
    pairs_per_split = sp_st // pair

    @pl.when(pl.program_id(0) == 0)
    def _():
        st_scr[...] = jnp.zeros_like(st_scr)

    u = u_ref[...].reshape(rows, d_ssm)
    ys = []
    for sp in range(n_split):
        bu_scr[:, sp * sp_st:(sp + 1) * sp_st] = _dot(u[:, sp * sp_in:(sp + 1) * sp_in], bblk_ref[sp])

        ms = range(sp * pairs_per_split, (sp + 1) * pairs_per_split)
        ar = [jnp.broadcast_to(are_ref[m:m + 1, :], (bsz, LANE)) for m in ms]
        ai = [jnp.broadcast_to(aim_ref[m:m + 1, :], (bsz, LANE)) for m in ms]
        state = [(st_scr[:, m * pair:m * pair + LANE], st_scr[:, m * pair + LANE:(m + 1) * pair])
                 for m in ms]
        for t in range(lt):
            r0 = t * bsz
            for j, m in enumerate(ms):
                xr, xi = state[j]
                c0 = m * pair
                nr = ar[j] * xr - ai[j] * xi + bu_scr[r0:r0 + bsz, c0:c0 + LANE]
                ni = ar[j] * xi + ai[j] * xr + bu_scr[r0:r0 + bsz, c0 + LANE:c0 + pair]
                x_scr[r0:r0 + bsz, c0:c0 + LANE] = nr.astype(BF16)
                x_scr[r0:r0 + bsz, c0 + LANE:c0 + pair] = ni.astype(BF16)
                state[j] = (nr, ni)
        for j, m in enumerate(ms):
            st_scr[:, m * pair:m * pair + LANE] = state[j][0]
            st_scr[:, m * pair + LANE:(m + 1) * pair] = state[j][1]

        ys.append(_dot(x_scr[:, sp * sp_st:(sp + 1) * sp_st], cblk_ref[sp]))
    y = jnp.concatenate(ys, axis=1) + d_ref[...] * u.astype(F32)
    y = jax.nn.gelu(y)
    z = _dot(y.astype(BF16), wglu_ref[...])
    o = z[:, :d_ssm] * jax.nn.sigmoid(z[:, d_ssm:])
    o = _rms(o) * g_ref[...]
    y_ref[...] = o.astype(BF16).reshape(lt, bsz, d_ssm)


def _s5(u_tm, bblk, are, aim, cblk, dvec, wglu, g, *, lt):
    seq, bsz, d_ssm = u_tm.shape
    n_state = bblk.shape[0] * bblk.shape[2]
    rows = lt * bsz
    c2 = lambda i: (0, 0)
    c3 = lambda i: (0, 0, 0)
    body = functools.partial(_s5_body, lt=lt, bsz=bsz, d_ssm=d_ssm)
    return pl.pallas_call(
        body,
        grid=(seq // lt,),
        in_specs=[pl.BlockSpec((lt, bsz, d_ssm), lambda i: (i, 0, 0)),
                  pl.BlockSpec(bblk.shape, c3),
                  pl.BlockSpec(are.shape, c2),
                  pl.BlockSpec(aim.shape, c2),
                  pl.BlockSpec(cblk.shape, c3),
                  pl.BlockSpec(dvec.shape, c2),
                  pl.BlockSpec(wglu.shape, c2),
                  pl.BlockSpec(g.shape, c2)],
        out_specs=pl.BlockSpec((lt, bsz, d_ssm), lambda i: (i, 0, 0)),
        out_shape=jax.ShapeDtypeStruct((seq, bsz, d_ssm), BF16),
        scratch_shapes=[pltpu.VMEM((rows, n_state), F32),
                        pltpu.VMEM((rows, n_state), BF16),
                        pltpu.VMEM((bsz, n_state), F32)],
        compiler_params=_params(1),
    )(u_tm, bblk, are, aim, cblk, dvec, wglu, g)


ATTN_HEAD_GROUP = 8


def _attn_body(q_ref, k_ref, vt_ref, g_ref, o_ref, s_scr, m_scr, acc_scr, *, tq, group):
    qi = pl.program_id(1)
    half = tq // 2
    contract_last = (((1,), (1,)), ((), ()))
    diag_parts = ((half, 0), (tq, half))

    for h0 in range(0, N_HEADS, group):
        heads = list(range(h0, h0 + group))

        def diag_block(heads=heads):
            k0 = pl.multiple_of(qi * tq, tq)
            for j, hh in enumerate(heads):
                sl = slice(hh * HEAD_PAD, (hh + 1) * HEAD_PAD)
                for nk, q0 in diag_parts:
                    s = lax.dot_general(k_ref[pl.ds(k0, nk), sl], q_ref[q0:q0 + half, sl], contract_last,
                                        preferred_element_type=F32)
                    key = lax.broadcasted_iota(jnp.int32, (nk, half), 0)
                    qry = lax.broadcasted_iota(jnp.int32, (nk, half), 1) + q0
                    s_scr[j, :nk, q0:q0 + half] = jnp.where(key <= qry, s, MASK_VALUE)
            for j, hh in enumerate(heads):
                sl = slice(hh * HEAD_PAD, (hh + 1) * HEAD_PAD)
                for nk, q0 in diag_parts:
                    s = s_scr[j, :nk, q0:q0 + half]
                    m_new = jnp.max(s, axis=0, keepdims=True)
                    p = jnp.exp2(s - m_new).astype(BF16)
                    vsl = slice(hh * V_ROWS, (hh + 1) * V_ROWS)
                    acc_scr[hh, :, q0:q0 + half] = _dot(vt_ref[qi, vsl, :nk], p)
                    m_scr[hh, :, q0:q0 + half] = m_new

        def full_block(ki, heads=heads):
            k0 = pl.multiple_of(ki * tq, tq)
            for j, hh in enumerate(heads):
                sl = slice(hh * HEAD_PAD, (hh + 1) * HEAD_PAD)
                s_scr[j] = lax.dot_general(k_ref[pl.ds(k0, tq), sl], q_ref[:, sl], contract_last,
                                           preferred_element_type=F32)
            for j, hh in enumerate(heads):
                sl = slice(hh * HEAD_PAD, (hh + 1) * HEAD_PAD)
                s = s_scr[j]
                m_prev = m_scr[hh]
                m_new = jnp.maximum(m_prev, jnp.max(s, axis=0, keepdims=True))
                p = jnp.exp2(s - m_new).astype(BF16)
                pv = _dot(vt_ref[ki, hh * V_ROWS:(hh + 1) * V_ROWS, :], p)
                acc_scr[hh] = jnp.exp2(m_prev - m_new) * acc_scr[hh] + pv
                m_scr[hh] = m_new

        diag_block()

        def kv_step(ki, carry, full_block=full_block):
            full_block(ki)
            return carry

        lax.fori_loop(0, qi, kv_step, 0)

    outs = []
    for hh in range(N_HEADS):
        acc = acc_scr[hh]
        outs.append(acc[:V_HEAD] / acc[V_HEAD:V_HEAD + 1])
    o_t = jnp.concatenate(outs, axis=0)
    o_t = o_t * lax.rsqrt(jnp.mean(o_t * o_t, axis=0, keepdims=True) + EPS)
    o_ref[...] = (o_t.T * g_ref[...]).astype(BF16)


def _attention(q, k, vt, g, *, tq, group):
    bsz, seq, hp = q.shape
    dv = N_HEADS * V_HEAD
    body = functools.partial(_attn_body, tq=tq, group=group)
    return pl.pallas_call(
        body,
        grid=(bsz, seq // tq),
        in_specs=[pl.BlockSpec((None, tq, hp), lambda b, i: (b, i, 0)),
                  pl.BlockSpec((None, seq, hp), lambda b, i: (b, 0, 0)),
                  pl.BlockSpec((None,) + vt.shape[1:], lambda b, i: (b, 0, 0, 0)),
                  pl.BlockSpec(g.shape, lambda b, i: (0, 0))],
        out_specs=pl.BlockSpec((None, tq, dv), lambda b, i: (b, i, 0)),
        out_shape=jax.ShapeDtypeStruct((bsz, seq, dv), BF16),
        scratch_shapes=[pltpu.VMEM((group, tq, tq), F32),
                        pltpu.VMEM((N_HEADS, 1, tq), F32),
                        pltpu.VMEM((N_HEADS, V_ROWS, tq), F32)],
        compiler_params=_params(2),
    )(q, k, vt, g)


FF_CHUNK = 1024
OUT_FFN_PARTS = 2


def _outffn_body(x_ref, ys_ref, ya_ref, mod_ref, fmod_ref, wos_ref, woa_ref, g2_ref, w1_ref, w2_ref,
                 gf_ref, o_ref, *, d_model):
    d = d_model
    mod = mod_ref[...]
    gate1 = mod[:, 2 * d:3 * d]
    shift2 = mod[:, 3 * d:4 * d]
    scale2 = mod[:, 4 * d:5 * d]
    gate2 = mod[:, 5 * d:6 * d]
    fmod = fmod_ref[...]
    fshift = fmod[:, :d]
    fscale = fmod[:, d:2 * d]
    d_ff = w1_ref.shape[1]
    tm = x_ref.shape[0]
    part = tm // OUT_FFN_PARTS
    def prologue(rows):
        mix = _dot(ys_ref[rows, :], wos_ref[...]) + _dot(ya_ref[rows, :], woa_ref[...])
        x1 = x_ref[rows, :] + gate1 * mix
        o_ref[rows, :] = x1
        return (_rms(x1) * g2_ref[...] * (1.0 + scale2) + shift2).astype(BF16)

    def ff_chunk(h, c0):
        a = jnp.maximum(_dot(h, w1_ref[:, c0:c0 + FF_CHUNK]), 0.0)
        return _dot((a * a).astype(BF16), w2_ref[c0:c0 + FF_CHUNK, :])

    def epilogue(rows, ff):
        x2 = o_ref[rows, :] + gate2 * ff
        o_ref[rows, :] = _rms(x2) * gf_ref[...] * (1.0 + fscale) + fshift

    ranges = [slice(r0, r0 + part) for r0 in range(0, tm, part)]
    chunks = list(range(0, d_ff, FF_CHUNK))
    h_cur = prologue(ranges[0])
    pending = None
    for idx, rows in enumerate(ranges):
        ff = ff_chunk(h_cur, chunks[0])
        if pending is not None:
            epilogue(*pending)
        h_next = prologue(ranges[idx + 1]) if idx + 1 < len(ranges) else None
        for c0 in chunks[1:]:
            ff = ff + ff_chunk(h_cur, c0)
        pending = (rows, ff)
        h_cur = h_next
    epilogue(*pending)


def _out_ffn(x, ys2d, ya, mod3, fmod3, wos, woa, g2, w1, w2, gf, *, tm):
    bsz, seq, d = x.shape
    d_half = ya.shape[2]
    body = functools.partial(_outffn_body, d_model=d)
    return pl.pallas_call(
        body,
        grid=(bsz, seq // tm),
        in_specs=[pl.BlockSpec((None, tm, d), lambda b, i: (b, i, 0)),
                  pl.BlockSpec((tm, d_half), lambda b, i: (i, b)),
                  pl.BlockSpec((None, tm, d_half), lambda b, i: (b, i, 0)),
                  pl.BlockSpec((None, 1, mod3.shape[2]), lambda b, i: (b, 0, 0)),
                  pl.BlockSpec((None, 1, fmod3.shape[2]), lambda b, i: (b, 0, 0)),
                  _resident(wos.shape),
                  _resident(woa.shape),
                  _resident(g2.shape),
                  _resident(w1.shape),
                  _resident(w2.shape),
                  _resident(gf.shape)],
        out_specs=pl.BlockSpec((None, tm, d), lambda b, i: (b, i, 0)),
        out_shape=jax.ShapeDtypeStruct((bsz, seq, d), F32),
        compiler_params=_params(2),
    )(x, ys2d, ya, mod3, fmod3, wos, woa, g2, w1, w2, gf)


def _pad_heads(w, head_in, pieces, width=HEAD_PAD):
    k = w.shape[0]
    w3 = w.reshape(k, N_HEADS, head_in)
    cols = []
    for piece in pieces:
        if isinstance(piece, int):
            cols.append(jnp.zeros((k, N_HEADS, piece), w.dtype))
        else:
            cols.append(piece(w3))
    out = jnp.concatenate(cols, axis=2)
    assert out.shape[2] == width
    return out.reshape(k, N_HEADS * width)


def _rot_half(wr):
    half = QK_ROPE // 2
    return jnp.concatenate([-wr[..., half:], wr[..., :half]], axis=-1)


def _s5_block_weights(bbar_re, bbar_im, c_re, c_im):
    g, h, p = bbar_re.shape
    n_pair = g // 2
    eye_m = jnp.eye(n_pair, dtype=F32)
    eye_g = jnp.eye(2, dtype=F32)
    bb = jnp.stack([bbar_re, bbar_im], axis=2).reshape(n_pair, 2, h, 2, p)
    b_full = jnp.einsum("mghrp,mn,gk->mghnrkp", bb, eye_m, eye_g).reshape(g * h, g * 2 * p)
    cc = jnp.stack([c_re, -c_im], axis=2).reshape(n_pair, 2, h, 2, p)
    c_full = jnp.einsum("mghrp,mn,gk->nrkpmgh", cc, eye_m, eye_g).reshape(g * 2 * p, g * h)
    n_kt = S5_SPLITS
    kin = (g * h) // n_kt
    kst = (g * 2 * p) // n_kt
    bblk = jnp.stack([b_full[kt * kin:(kt + 1) * kin, kt * kst:(kt + 1) * kst] for kt in range(n_kt)])
    cblk = jnp.stack([c_full[kt * kst:(kt + 1) * kst, kt * kin:(kt + 1) * kin] for kt in range(n_kt)])
    return bblk.astype(BF16), cblk.astype(BF16)


def _forward(x, c, positions, ada_w, ada_b, norm1_g, w_in, ssm_lambda_re, ssm_lambda_im,
             ssm_b_re, ssm_b_im, ssm_c_re, ssm_c_im, ssm_d, ssm_log_dt, w_glu,
             q_norm_g, w_uq, kv_norm_g, w_ukv, ssm_out_g, attn_out_g, w_out,
             norm2_g, w_ff1, w_ff2, final_ada_w, final_ada_b, final_norm_g, *, tm, tq, lt):
    bsz, seq, d = x.shape
    depth = ada_w.shape[0]
    assert depth == 1, "the fused epilogue applies the final norm right after the only layer"
    assert tm == tq, "in_proj emits one V^T tile per attention key block"
    d_ssm = w_glu.shape[1]
    q_lora = w_uq.shape[1]
    kv_lora = w_ukv.shape[1]
    s2 = d_ssm + q_lora
    s3 = s2 + kv_lora

    inv_freq = ROPE_BASE ** (-jnp.arange(0, QK_ROPE, 2, dtype=F32) / QK_ROPE)
    t1, t2 = _rope_tables(positions.astype(F32), inv_freq)

    fmod = _modulation(c, final_ada_w, final_ada_b)
    fmod3 = fmod.reshape(bsz, 1, fmod.shape[1])

    for l in range(depth):
        mod = _modulation(c, ada_w[l], ada_b[l])
        mod3 = mod.reshape(bsz, 1, mod.shape[1])

        wi = w_in[l]
        wkr = wi[:, s3:]
        win = jnp.concatenate([wi[:, :s3], jnp.zeros((d, QK_NOPE), F32), wkr, _rot_half(wkr)],
                              axis=1).astype(BF16)
        hq = QK_NOPE + QK_ROPE
        wuq = _pad_heads(w_uq[l], hq, [lambda w3: w3[..., :hq],
                                       lambda w3: _rot_half(w3[..., QK_NOPE:])]).astype(BF16)
        wk = _pad_heads(w_ukv[l], QK_NOPE + V_HEAD,
                        [lambda w3: w3[..., :QK_NOPE], HEAD_PAD - QK_NOPE]).astype(BF16)
        wv = _pad_heads(w_ukv[l], QK_NOPE + V_HEAD,
                        [lambda w3: w3[..., QK_NOPE:], V_ROWS - V_HEAD], width=V_ROWS).T.astype(BF16)

        u2d, q, k, v = _in_proj(x, mod3, norm1_g[l].reshape(1, d), win, q_norm_g[l].reshape(1, q_lora),
                                wuq, kv_norm_g[l].reshape(1, kv_lora), wk, wv, t1, t2,
                                tm=tm, d_ssm=d_ssm, q_lora=q_lora, kv_lora=kv_lora)

        are, aim, bbar_re, bbar_im = _discretize(
            ssm_lambda_re[l], ssm_lambda_im[l], ssm_log_dt[l],
            jnp.transpose(ssm_b_re[l], (0, 2, 1)), jnp.transpose(ssm_b_im[l], (0, 2, 1)))
        bblk, cblk = _s5_block_weights(bbar_re, bbar_im, ssm_c_re[l], ssm_c_im[l])
        n_pair = are.shape[0] // 2
        ys_tm = _s5(u2d.reshape(seq, bsz, d_ssm), bblk, are.reshape(n_pair, LANE), aim.reshape(n_pair, LANE),
                    cblk, ssm_d[l].reshape(1, d_ssm), w_glu[l].astype(BF16), ssm_out_g[l].reshape(1, d_ssm),
                    lt=lt)

        ya = _attention(q, k, v, attn_out_g[l].reshape(1, -1), tq=tq, group=ATTN_HEAD_GROUP)

        wo = w_out[l].astype(BF16)
        out = _out_ffn(x, ys_tm.reshape(seq, bsz * d_ssm), ya, mod3, fmod3, wo[:d_ssm], wo[d_ssm:],
                       norm2_g[l].reshape(1, d), w_ff1[l].astype(BF16), w_ff2[l].astype(BF16),
                       final_norm_g.reshape(1, d), tm=2 * tm)
        x = out
    return x


def kernel(x, c, positions, ada_w, ada_b, norm1_g, w_in, ssm_lambda_re, ssm_lambda_im, ssm_b_re, ssm_b_im, ssm_c_re, ssm_c_im, ssm_d, ssm_log_dt, w_glu, q_norm_g, w_uq, kv_norm_g, w_ukv, ssm_out_g, attn_out_g, w_out, norm2_g, w_ff1, w_ff2, final_ada_w, final_ada_b, final_norm_g):
    return _forward(x, c, positions, ada_w, ada_b, norm1_g, w_in, ssm_lambda_re, ssm_lambda_im,
                    ssm_b_re, ssm_b_im, ssm_c_re, ssm_c_im, ssm_d, ssm_log_dt, w_glu,
                    q_norm_g, w_uq, kv_norm_g, w_ukv, ssm_out_g, attn_out_g, w_out,
                    norm2_g, w_ff1, w_ff2, final_ada_w, final_ada_b, final_norm_g,
                    tm=512, tq=512, lt=32)
```

```python
import functools
import math

import jax
import jax.numpy as jnp
from jax import lax
from jax.experimental import pallas as pl
from jax.experimental.pallas import tpu as pltpu

F32 = jnp.float32
BF16 = jnp.bfloat16

SSM_GROUP = 16
SSM_STATE = 64
N_HEADS = 8
QK_NOPE = 64
QK_ROPE = 32
V_HEAD = 64
ROPE_BASE = 10000.0
EPS = 1e-6
LANE = 128
HEAD_PAD = LANE
V_ROWS = HEAD_PAD
IN_PROJ_PARTS = 2
MASK_VALUE = -1e30
VMEM_LIMIT = 56 * 1024 * 1024


def _rms(x):
    return x * lax.rsqrt(jnp.mean(x * x, axis=-1, keepdims=True) + EPS)


def _dot(a, b):
    return jnp.dot(a, b, preferred_element_type=F32)


def _resident(shape):
    zeros = (0,) * len(shape)
    return pl.BlockSpec(shape, lambda *_: zeros, pipeline_mode=pl.Buffered(1))


def _params(n_grid_dims):
    return pltpu.CompilerParams(dimension_semantics=("arbitrary",) * n_grid_dims,
                                vmem_limit_bytes=VMEM_LIMIT)


def _mod_body(c_ref, w_ref, b_ref, o_ref):
    c = c_ref[...]
    cond = c * jax.nn.sigmoid(c)
    o_ref[...] = jnp.dot(cond, w_ref[...], preferred_element_type=F32,
                         precision=lax.Precision.HIGHEST) + b_ref[...]


def _modulation(c, w, b, block_n=1024):
    bsz, d = c.shape
    n = w.shape[1]
    return pl.pallas_call(
        _mod_body,
        grid=(n // block_n,),
        in_specs=[pl.BlockSpec((bsz, d), lambda j: (0, 0)),
                  pl.BlockSpec((d, block_n), lambda j: (0, j)),
                  pl.BlockSpec((1, block_n), lambda j: (0, j))],
        out_specs=pl.BlockSpec((bsz, block_n), lambda j: (0, j)),
        out_shape=jax.ShapeDtypeStruct((bsz, n), F32),
        compiler_params=_params(1),
    )(c, w, b.reshape(1, n))


def _disc_body(lre_ref, lim_ref, ldt_ref, bre_ref, bim_ref, are_ref, aim_ref, obre_ref, obim_ref):
    lre = lre_ref[...]
    lim = lim_ref[...]
    dt = jnp.exp(ldt_ref[...])
    mag = jnp.exp(lre * dt)
    are = mag * jnp.cos(lim * dt)
    aim = mag * jnp.sin(lim * dt)
    are_ref[...] = are
    aim_ref[...] = aim
    nre = are - 1.0
    den = lre * lre + lim * lim
    cre = (nre * lre + aim * lim) / den
    cim = (aim * lre - nre * lim) / den
    bre = bre_ref[...]
    bim = bim_ref[...]
    obre_ref[...] = cre * bre - cim * bim
    obim_ref[...] = cre * bim + cim * bre


def _discretize(lam_re, lam_im, log_dt, b_re_t, b_im_t):
    g, p = lam_re.shape
    h = b_re_t.shape[1]
    are, aim, bbar_re, bbar_im = pl.pallas_call(
        _disc_body,
        out_shape=(jax.ShapeDtypeStruct((g, 1, p), F32), jax.ShapeDtypeStruct((g, 1, p), F32),
                   jax.ShapeDtypeStruct((g, h, p), F32), jax.ShapeDtypeStruct((g, h, p), F32)),
    )(lam_re.reshape(g, 1, p), lam_im.reshape(g, 1, p), log_dt.reshape(g, 1, 1), b_re_t, b_im_t)
    return are.reshape(g, p), aim.reshape(g, p), bbar_re, bbar_im


def _rope_body(pos_ref, f_ref, t1_ref, t2_ref):
    ang = f_ref[...] * pos_ref[...]
    cos = jnp.cos(ang)
    sin = jnp.sin(ang)
    half = QK_ROPE // 2
    seq = ang.shape[1]
    t1_ref[:QK_NOPE, :] = jnp.ones((QK_NOPE, seq), F32)
    t2_ref[:QK_NOPE, :] = jnp.zeros((QK_NOPE, seq), F32)
    for r0 in (QK_NOPE, QK_NOPE + half):
        t1_ref[r0:r0 + half, :] = cos
        t2_ref[r0:r0 + half, :] = sin
    pad = HEAD_PAD - QK_NOPE - QK_ROPE
    t1_ref[HEAD_PAD - pad:, :] = jnp.zeros((pad, seq), F32)
    t2_ref[HEAD_PAD - pad:, :] = jnp.zeros((pad, seq), F32)


def _rope_tables(posf, inv_freq):
    bsz, seq = posf.shape
    nf = inv_freq.shape[0]
    out = jax.ShapeDtypeStruct((bsz, HEAD_PAD, seq), F32)
    return pl.pallas_call(
        _rope_body,
        grid=(bsz,),
        in_specs=[pl.BlockSpec((None, 1, seq), lambda b: (b, 0, 0)),
                  pl.BlockSpec((nf, 1), lambda b: (0, 0))],
        out_specs=(pl.BlockSpec((None, HEAD_PAD, seq), lambda b: (b, 0, 0)),
                   pl.BlockSpec((None, HEAD_PAD, seq), lambda b: (b, 0, 0))),
        out_shape=(out, out),
        compiler_params=_params(1),
    )(posf.reshape(bsz, 1, seq), inv_freq.reshape(nf, 1))


def _inproj_body(x_ref, mod_ref, sw_ref, g1_ref, win_ref, qg_ref, wuq_ref, kvg_ref, wk_ref, wv_ref,
                 t1_ref, t2_ref, u_ref, q_ref, k_ref, v_ref, *, d_model, d_ssm, q_lora, kv_lora, scale):
    col_scale = g1_ref[...] * (1.0 + mod_ref[:, d_model:2 * d_model])
    s2 = d_ssm + q_lora
    s3 = s2 + kv_lora
    tm = x_ref.shape[0]
    part = tm // IN_PROJ_PARTS
    row = lax.broadcasted_iota(jnp.int32, (v_ref.shape[1], 1), 0)
    ones_rows = (row % V_ROWS == V_HEAD).astype(F32)
    for r0 in range(0, tm, part):
        rows = slice(r0, r0 + part)
        x = x_ref[rows, :]
        r_x = lax.rsqrt(jnp.mean(x * x, axis=-1, keepdims=True) + EPS)
        proj = _dot((x * col_scale).astype(BF16), win_ref[...]) * r_x + sw_ref[...]
        u_ref[rows, :] = proj[:, :d_ssm].astype(BF16)

        ql = proj[:, d_ssm:s2]
        r_q = lax.rsqrt(jnp.mean(ql * ql, axis=-1, keepdims=True) + EPS) * scale
        q = _dot((ql * qg_ref[...]).astype(BF16), wuq_ref[...])
        kvn = (_rms(proj[:, s2:s3]) * kvg_ref[...]).astype(BF16)
        kn = _dot(kvn, wk_ref[...])
        v_t = lax.dot_general(wv_ref[...], kvn, (((1,), (1,)), ((), ())), preferred_element_type=F32)
        v_ref[r0 // part] = (v_t + ones_rows).astype(BF16)

        t1 = t1_ref[:, rows].T
        t2 = t2_ref[:, rows].T
        kr = proj[:, s3:s3 + HEAD_PAD]
        kr = kr * t1 + pltpu.roll(kr, HEAD_PAD - QK_ROPE, 1) * t2
        for hh in range(N_HEADS):
            sl = slice(hh * HEAD_PAD, (hh + 1) * HEAD_PAD)
            qh = q[:, sl]
            qh = qh * t1 + pltpu.roll(qh, HEAD_PAD - QK_ROPE, 1) * t2
            q_ref[rows, sl] = (qh * r_q).astype(BF16)
            k_ref[rows, sl] = (kn[:, sl] + kr).astype(BF16)


def _shift_proj_body(s_ref, w_ref, o_ref):
    o_ref[...] = _dot(s_ref[...].astype(BF16), w_ref[...])


def _shift_proj(shift, w):
    out = pl.pallas_call(
        _shift_proj_body,
        out_shape=jax.ShapeDtypeStruct((shift.shape[0], w.shape[1]), F32),
        compiler_params=pltpu.CompilerParams(vmem_limit_bytes=VMEM_LIMIT),
    )(shift, w)
    return out.reshape(shift.shape[0], 1, w.shape[1])


def _in_proj(x, mod3, sw3, g1, win, qg, wuq, kvg, wk, wv, t1, t2, *, tm, d_ssm, q_lora, kv_lora):
    bsz, seq, d = x.shape
    tkv = tm
    tm = IN_PROJ_PARTS * tkv
    nt = seq // tm
    hp = N_HEADS * HEAD_PAD
    scale = float((QK_NOPE + QK_ROPE) ** -0.5 * math.log2(math.e))
    const = lambda b, i: (0, 0)
    body = functools.partial(_inproj_body, d_model=d, d_ssm=d_ssm, q_lora=q_lora, kv_lora=kv_lora,
                             scale=scale)
    return pl.pallas_call(
        body,
        grid=(bsz, nt),
        in_specs=[pl.BlockSpec((None, tm, d), lambda b, i: (b, i, 0)),
                  pl.BlockSpec((None, 1, mod3.shape[2]), lambda b, i: (b, 0, 0)),
                  pl.BlockSpec((None, 1, sw3.shape[2]), lambda b, i: (b, 0, 0)),
                  pl.BlockSpec(g1.shape, const),
                  pl.BlockSpec(win.shape, const),
                  pl.BlockSpec(qg.shape, const),
                  pl.BlockSpec(wuq.shape, const),
                  pl.BlockSpec(kvg.shape, const),
                  pl.BlockSpec(wk.shape, const),
                  pl.BlockSpec(wv.shape, const),
                  pl.BlockSpec((None, HEAD_PAD, tm), lambda b, i: (b, 0, i)),
                  pl.BlockSpec((None, HEAD_PAD, tm), lambda b, i: (b, 0, i))],
        out_specs=(pl.BlockSpec((tm, d_ssm), lambda b, i: (i, b)),
                   pl.BlockSpec((None, tm, hp), lambda b, i: (b, i, 0)),
                   pl.BlockSpec((None, tm, hp), lambda b, i: (b, i, 0)),
                   pl.BlockSpec((None, IN_PROJ_PARTS, wv.shape[0], tkv), lambda b, i: (b, i, 0, 0))),
        out_shape=(jax.ShapeDtypeStruct((seq, bsz * d_ssm), BF16),
                   jax.ShapeDtypeStruct((bsz, seq, hp), BF16),
                   jax.ShapeDtypeStruct((bsz, seq, hp), BF16),
                   jax.ShapeDtypeStruct((bsz, seq // tkv, wv.shape[0], tkv), BF16)),
        compiler_params=_params(2),
    )(x, mod3, sw3, g1, win, qg, wuq, kvg, wk, wv, t1, t2)


S5_SPLITS = 4


def _s5_body(u_ref, bblk_ref, are_ref, aim_ref, cblk_ref, d_ref, wglu_ref, g_ref, y_ref,
             bu_scr, x_scr, st_scr, *, lt, bsz, d_ssm):
    rows = lt * bsz
    n_split = bblk_ref.shape[0]
    sp_in = bblk_ref.shape[1]
    sp_st = bblk_ref.shape[2]
    pair = 2 * LANE
    pairs_per_split = sp_st // pair

    @pl.when(pl.program_id(0) == 0)
    def _():
        st_scr[...] = jnp.zeros_like(st_scr)

    u = u_ref[...].reshape(rows, d_ssm)
    ys = []
    for sp in range(n_split):
        bu_scr[:, sp * sp_st:(sp + 1) * sp_st] = _dot(u[:, sp * sp_in:(sp + 1) * sp_in], bblk_ref[sp])

        ms = range(sp * pairs_per_split, (sp + 1) * pairs_per_split)
        ar = [jnp.broadcast_to(are_ref[m:m + 1, :], (bsz, LANE)) for m in ms]
        ai = [jnp.broadcast_to(aim_ref[m:m + 1, :], (bsz, LANE)) for m in ms]
        state = [(st_scr[:, m * pair:m * pair + LANE], st_scr[:, m * pair + LANE:(m + 1) * pair])
                 for m in ms]
        for t in range(lt):
            r0 = t * bsz
            for j, m in enumerate(ms):
                xr, xi = state[j]
                c0 = m * pair
                nr = ar[j] * xr - ai[j] * xi + bu_scr[r0:r0 + bsz, c0:c0 + LANE]
                ni = ar[j] * xi + ai[j] * xr + bu_scr[r0:r0 + bsz, c0 + LANE:c0 + pair]
                x_scr[r0:r0 + bsz, c0:c0 + LANE] = nr.astype(BF16)
                x_scr[r0:r0 + bsz, c0 + LANE:c0 + pair] = ni.astype(BF16)
                state[j] = (nr, ni)
        for j, m in enumerate(ms):
            st_scr[:, m * pair:m * pair + LANE] = state[j][0]
            st_scr[:, m * pair + LANE:(m + 1) * pair] = state[j][1]

        ys.append(_dot(x_scr[:, sp * sp_st:(sp + 1) * sp_st], cblk_ref[sp]))
    y = jnp.concatenate(ys, axis=1) + d_ref[...] * u.astype(F32)
    y = jax.nn.gelu(y)
    z = _dot(y.astype(BF16), wglu_ref[...])
    o = z[:, :d_ssm] * jax.nn.sigmoid(z[:, d_ssm:])
    o = _rms(o) * g_ref[...]
    y_ref[...] = o.astype(BF16).reshape(lt, bsz, d_ssm)


def _s5(u_tm, bblk, are, aim, cblk, dvec, wglu, g, *, lt):
    seq, bsz, d_ssm = u_tm.shape
    n_state = bblk.shape[0] * bblk.shape[2]
    rows = lt * bsz
    c2 = lambda i: (0, 0)
    c3 = lambda i: (0, 0, 0)
    body = functools.partial(_s5_body, lt=lt, bsz=bsz, d_ssm=d_ssm)
    return pl.pallas_call(
        body,
        grid=(seq // lt,),
        in_specs=[pl.BlockSpec((lt, bsz, d_ssm), lambda i: (i, 0, 0)),
                  pl.BlockSpec(bblk.shape, c3),
                  pl.BlockSpec(are.shape, c2),
                  pl.BlockSpec(aim.shape, c2),
                  pl.BlockSpec(cblk.shape, c3),
                  pl.BlockSpec(dvec.shape, c2),
                  pl.BlockSpec(wglu.shape, c2),
                  pl.BlockSpec(g.shape, c2)],
        out_specs=pl.BlockSpec((lt, bsz, d_ssm), lambda i: (i, 0, 0)),
        out_shape=jax.ShapeDtypeStruct((seq, bsz, d_ssm), BF16),
        scratch_shapes=[pltpu.VMEM((rows, n_state), F32),
                        pltpu.VMEM((rows, n_state), BF16),
                        pltpu.VMEM((bsz, n_state), F32)],
        compiler_params=_params(1),
    )(u_tm, bblk, are, aim, cblk, dvec, wglu, g)


ATTN_HEAD_GROUP = 8


def _attn_body(q_ref, k_ref, vt_ref, g_ref, o_ref, s_scr, m_scr, acc_scr, *, tq, group):
    qi = pl.program_id(1)
    half = tq // 2
    contract_last = (((1,), (1,)), ((), ()))
    diag_parts = ((half, 0), (tq, half))

    for h0 in range(0, N_HEADS, group):
        heads = list(range(h0, h0 + group))

        def diag_block(heads=heads):
            k0 = pl.multiple_of(qi * tq, tq)
            for j, hh in enumerate(heads):
                sl = slice(hh * HEAD_PAD, (hh + 1) * HEAD_PAD)
                for nk, q0 in diag_parts:
                    s = lax.dot_general(k_ref[pl.ds(k0, nk), sl], q_ref[q0:q0 + half, sl], contract_last,
                                        preferred_element_type=F32)
                    key = lax.broadcasted_iota(jnp.int32, (nk, half), 0)
                    qry = lax.broadcasted_iota(jnp.int32, (nk, half), 1) + q0
                    s_scr[j, :nk, q0:q0 + half] = jnp.where(key <= qry, s, MASK_VALUE)
            for j, hh in enumerate(heads):
                sl = slice(hh * HEAD_PAD, (hh + 1) * HEAD_PAD)
                for nk, q0 in diag_parts:
                    s = s_scr[j, :nk, q0:q0 + half]
                    m_new = jnp.max(s, axis=0, keepdims=True)
                    p = jnp.exp2(s - m_new).astype(BF16)
                    vsl = slice(hh * V_ROWS, (hh + 1) * V_ROWS)
                    acc_scr[hh, :, q0:q0 + half] = _dot(vt_ref[qi, vsl, :nk], p)
                    m_scr[hh, :, q0:q0 + half] = m_new

        def full_block(ki, heads=heads):
            k0 = pl.multiple_of(ki * tq, tq)
            for j, hh in enumerate(heads):
                sl = slice(hh * HEAD_PAD, (hh + 1) * HEAD_PAD)
                s_scr[j] = lax.dot_general(k_ref[pl.ds(k0, tq), sl], q_ref[:, sl], contract_last,
                                           preferred_element_type=F32)
            for j, hh in enumerate(heads):
                sl = slice(hh * HEAD_PAD, (hh + 1) * HEAD_PAD)
                s = s_scr[j]
                m_prev = m_scr[hh]
                m_new = jnp.maximum(m_prev, jnp.max(s, axis=0, keepdims=True))
                p = jnp.exp2(s - m_new).astype(BF16)
                pv = _dot(vt_ref[ki, hh * V_ROWS:(hh + 1) * V_ROWS, :], p)
                acc_scr[hh] = jnp.exp2(m_prev - m_new) * acc_scr[hh] + pv
                m_scr[hh] = m_new

        diag_block()

        def kv_step(ki, carry, full_block=full_block):
            full_block(ki)
            return carry

        lax.fori_loop(0, qi, kv_step, 0)

    outs = []
    for hh in range(N_HEADS):
        acc = acc_scr[hh]
        outs.append(acc[:V_HEAD] / acc[V_HEAD:V_HEAD + 1])
    o_t = jnp.concatenate(outs, axis=0)
    o_t = o_t * lax.rsqrt(jnp.mean(o_t * o_t, axis=0, keepdims=True) + EPS)
    o_ref[...] = (o_t.T * g_ref[...]).astype(BF16)


def _attention(q, k, vt, g, *, tq, group):
    bsz, seq, hp = q.shape
    dv = N_HEADS * V_HEAD
    body = functools.partial(_attn_body, tq=tq, group=group)
    return pl.pallas_call(
        body,
        grid=(bsz, seq // tq),
        in_specs=[pl.BlockSpec((None, tq, hp), lambda b, i: (b, i, 0)),
                  pl.BlockSpec((None, seq, hp), lambda b, i: (b, 0, 0)),
                  pl.BlockSpec((None,) + vt.shape[1:], lambda b, i: (b, 0, 0, 0)),
                  pl.BlockSpec(g.shape, lambda b, i: (0, 0))],
        out_specs=pl.BlockSpec((None, tq, dv), lambda b, i: (b, i, 0)),
        out_shape=jax.ShapeDtypeStruct((bsz, seq, dv), BF16),
        scratch_shapes=[pltpu.VMEM((group, tq, tq), F32),
                        pltpu.VMEM((N_HEADS, 1, tq), F32),
                        pltpu.VMEM((N_HEADS, V_ROWS, tq), F32)],
        compiler_params=_params(2),
    )(q, k, vt, g)


FF_CHUNK = 1024
OUT_FFN_PARTS = 2


def _outffn_body(x_ref, ys_ref, ya_ref, mod_ref, fmod_ref, wos_ref, woa_ref, g2_ref, w1_ref, w2_ref,
                 gf_ref, o_ref, *, d_model):
    d = d_model
    mod = mod_ref[...]
    gate1 = mod[:, 2 * d:3 * d]
    shift2 = mod[:, 3 * d:4 * d]
    scale2 = mod[:, 4 * d:5 * d]
    gate2 = mod[:, 5 * d:6 * d]
    fmod = fmod_ref[...]
    fshift = fmod[:, :d]
    fscale = fmod[:, d:2 * d]
    d_ff = w1_ref.shape[1]
    tm = x_ref.shape[0]
    part = tm // OUT_FFN_PARTS

    def prologue(rows):
        mix = _dot(ys_ref[rows, :], wos_ref[...]) + _dot(ya_ref[rows, :], woa_ref[...])
        x1 = x_ref[rows, :] + gate1 * mix
        o_ref[rows, :] = x1
        return (_rms(x1) * g2_ref[...] * (1.0 + scale2) + shift2).astype(BF16)

    def ff_chunk(h, c0):
        a = jnp.maximum(_dot(h, w1_ref[:, c0:c0 + FF_CHUNK]), 0.0)
        return _dot((a * a).astype(BF16), w2_ref[c0:c0 + FF_CHUNK, :])

    def epilogue(rows, ff):
        x2 = o_ref[rows, :] + gate2 * ff
        o_ref[rows, :] = _rms(x2) * gf_ref[...] * (1.0 + fscale) + fshift

    ranges = [slice(r0, r0 + part) for r0 in range(0, tm, part)]
    chunks = list(range(0, d_ff, FF_CHUNK))
    h_cur = prologue(ranges[0])
    pending = None
    for idx, rows in enumerate(ranges):
        ff = ff_chunk(h_cur, chunks[0])
        if pending is not None:
            epilogue(*pending)
        h_next = prologue(ranges[idx + 1]) if idx + 1 < len(ranges) else None
        for c0 in chunks[1:]:
            ff = ff + ff_chunk(h_cur, c0)
        pending = (rows, ff)
        h_cur = h_next
    epilogue(*pending)


def _out_ffn(x, ys2d, ya, mod3, fmod3, wos, woa, g2, w1, w2, gf, *, tm):
    bsz, seq, d = x.shape
    d_half = ya.shape[2]
    body = functools.partial(_outffn_body, d_model=d)
    return pl.pallas_call(
        body,
        grid=(bsz, seq // tm),
        in_specs=[pl.BlockSpec((None, tm, d), lambda b, i: (b, i, 0)),
                  pl.BlockSpec((tm, d_half), lambda b, i: (i, b)),
                  pl.BlockSpec((None, tm, d_half), lambda b, i: (b, i, 0)),
                  pl.BlockSpec((None, 1, mod3.shape[2]), lambda b, i: (b, 0, 0)),
                  pl.BlockSpec((None, 1, fmod3.shape[2]), lambda b, i: (b, 0, 0)),
                  _resident(wos.shape),
                  _resident(woa.shape),
                  _resident(g2.shape),
                  _resident(w1.shape),
                  _resident(w2.shape),
                  _resident(gf.shape)],
        out_specs=pl.BlockSpec((None, tm, d), lambda b, i: (b, i, 0)),
        out_shape=jax.ShapeDtypeStruct((bsz, seq, d), F32),
        compiler_params=_params(2),
    )(x, ys2d, ya, mod3, fmod3, wos, woa, g2, w1, w2, gf)


def _pad_heads(w, head_in, pieces, width=HEAD_PAD):
    k = w.shape[0]
    w3 = w.reshape(k, N_HEADS, head_in)
    cols = []
    for piece in pieces:
        if isinstance(piece, int):
            cols.append(jnp.zeros((k, N_HEADS, piece), w.dtype))
        else:
            cols.append(piece(w3))
    out = jnp.concatenate(cols, axis=2)
    assert out.shape[2] == width
    return out.reshape(k, N_HEADS * width)


def _rot_half(wr):
    half = QK_ROPE // 2
    return jnp.concatenate([-wr[..., half:], wr[..., :half]], axis=-1)


def _s5_block_weights(bbar_re, bbar_im, c_re, c_im):
    g, h, p = bbar_re.shape
    gs = g // S5_SPLITS
    eye = jnp.eye(gs, dtype=F32)
    bb = jnp.stack([bbar_re, bbar_im], axis=2).reshape(S5_SPLITS, gs, h, 1, 2, p)
    b6 = bb * eye.reshape(1, gs, 1, gs, 1, 1)
    b6 = b6.reshape(S5_SPLITS, gs, h, gs // 2, 2, 2, p).transpose(0, 1, 2, 3, 5, 4, 6)
    bblk = b6.reshape(S5_SPLITS, gs * h, gs * 2 * p)
    cc = jnp.stack([c_re, -c_im], axis=2).reshape(S5_SPLITS, 1, gs, h, 2, p)
    c6 = cc * eye.reshape(1, gs, gs, 1, 1, 1)
    c6 = c6.reshape(S5_SPLITS, gs // 2, 2, gs, h, 2, p).transpose(0, 1, 5, 2, 6, 3, 4)
    cblk = c6.reshape(S5_SPLITS, gs * 2 * p, gs * h)
    return bblk.astype(BF16), cblk.astype(BF16)


def _forward(x, c, positions, ada_w, ada_b, norm1_g, w_in, ssm_lambda_re, ssm_lambda_im,
             ssm_b_re, ssm_b_im, ssm_c_re, ssm_c_im, ssm_d, ssm_log_dt, w_glu,
             q_norm_g, w_uq, kv_norm_g, w_ukv, ssm_out_g, attn_out_g, w_out,
             norm2_g, w_ff1, w_ff2, final_ada_w, final_ada_b, final_norm_g, *, tm, tq, lt):
    bsz, seq, d = x.shape
    depth = ada_w.shape[0]
    assert depth == 1, "the fused epilogue applies the final norm right after the only layer"
    assert tm == tq, "in_proj emits one V^T tile per attention key block"
    d_ssm = w_glu.shape[1]
    q_lora = w_uq.shape[1]
    kv_lora = w_ukv.shape[1]
    s2 = d_ssm + q_lora
    s3 = s2 + kv_lora

    inv_freq = ROPE_BASE ** (-jnp.arange(0, QK_ROPE, 2, dtype=F32) / QK_ROPE)
    t1, t2 = _rope_tables(positions.astype(F32), inv_freq)

    fmod = _modulation(c, final_ada_w, final_ada_b)
    fmod3 = fmod.reshape(bsz, 1, fmod.shape[1])

    for l in range(depth):
        mod = _modulation(c, ada_w[l], ada_b[l])
        mod3 = mod.reshape(bsz, 1, mod.shape[1])

        wi = w_in[l]
        wkr = wi[:, s3:]
        win = jnp.concatenate([wi[:, :s3], jnp.zeros((d, QK_NOPE), F32), wkr, _rot_half(wkr)],
                              axis=1).astype(BF16)
        hq = QK_NOPE + QK_ROPE
        wuq = _pad_heads(w_uq[l], hq, [lambda w3: w3[..., :hq],
                                       lambda w3: _rot_half(w3[..., QK_NOPE:])]).astype(BF16)
        wk = _pad_heads(w_ukv[l], QK_NOPE + V_HEAD,
                        [lambda w3: w3[..., :QK_NOPE], HEAD_PAD - QK_NOPE]).astype(BF16)
        wv = _pad_heads(w_ukv[l], QK_NOPE + V_HEAD,
                        [lambda w3: w3[..., QK_NOPE:], V_ROWS - V_HEAD], width=V_ROWS).T.astype(BF16)

        u2d, q, k, v = _in_proj(x, mod3, _shift_proj(mod[:, :d], win),
                                norm1_g[l].reshape(1, d), win, q_norm_g[l].reshape(1, q_lora),
                                wuq, kv_norm_g[l].reshape(1, kv_lora), wk, wv, t1, t2,
                                tm=tm, d_ssm=d_ssm, q_lora=q_lora, kv_lora=kv_lora)

        are, aim, bbar_re, bbar_im = _discretize(
            ssm_lambda_re[l], ssm_lambda_im[l], ssm_log_dt[l],
            jnp.transpose(ssm_b_re[l], (0, 2, 1)), jnp.transpose(ssm_b_im[l], (0, 2, 1)))
        bblk, cblk = _s5_block_weights(bbar_re, bbar_im, ssm_c_re[l], ssm_c_im[l])
        n_pair = are.shape[0] // 2
        ys_tm = _s5(u2d.reshape(seq, bsz, d_ssm), bblk, are.reshape(n_pair, LANE), aim.reshape(n_pair, LANE),
                    cblk, ssm_d[l].reshape(1, d_ssm), w_glu[l].astype(BF16), ssm_out_g[l].reshape(1, d_ssm),
                    lt=lt)

        ya = _attention(q, k, v, attn_out_g[l].reshape(1, -1), tq=tq, group=ATTN_HEAD_GROUP)

        wo = w_out[l].astype(BF16)
        out = _out_ffn(x, ys_tm.reshape(seq, bsz * d_ssm), ya, mod3, fmod3, wo[:d_ssm], wo[d_ssm:],
                       norm2_g[l].reshape(1, d), w_ff1[l].astype(BF16), w_ff2[l].astype(BF16),
                       final_norm_g.reshape(1, d), tm=2 * tm)
        x = out
    return x


def kernel(x, c, positions, ada_w, ada_b, norm1_g, w_in, ssm_lambda_re, ssm_lambda_im, ssm_b_re, ssm_b_im, ssm_c_re, ssm_c_im, ssm_d, ssm_log_dt, w_glu, q_norm_g, w_uq, kv_norm_g, w_ukv, ssm_out_g, attn_out_g, w_out, norm2_g, w_ff1, w_ff2, final_ada_w, final_ada_b, final_norm_g):
    return _forward(x, c, positions, ada_w, ada_b, norm1_g, w_in, ssm_lambda_re, ssm_lambda_im,
                    ssm_b_re, ssm_b_im, ssm_c_re, ssm_c_im, ssm_d, ssm_log_dt, w_glu,
                    q_norm_g, w_uq, kv_norm_g, w_ukv, ssm_out_g, attn_out_g, w_out,
                    norm2_g, w_ff1, w_ff2, final_ada_w, final_ada_b, final_norm_g,
                    tm=512, tq=512, lt=32)
```

```python
import functools
import math

import jax
import jax.numpy as jnp
from jax import lax
from jax.experimental import pallas as pl
from jax.experimental.pallas import tpu as pltpu

F32 = jnp.float32
BF16 = jnp.bfloat16

SSM_GROUP = 16
SSM_STATE = 64
N_HEADS = 8
QK_NOPE = 64
QK_ROPE = 32
V_HEAD = 64
ROPE_BASE = 10000.0
EPS = 1e-6
LANE = 128
HEAD_PAD = LANE
V_ROWS = HEAD_PAD
IN_PROJ_PARTS = 2
MASK_VALUE = -1e30
VMEM_LIMIT = 56 * 1024 * 1024


def _rms(x):
    return x * lax.rsqrt(jnp.mean(x * x, axis=-1, keepdims=True) + EPS)


def _dot(a, b):
    return jnp.dot(a, b, preferred_element_type=F32)


def _resident(shape):
    zeros = (0,) * len(shape)
    return pl.BlockSpec(shape, lambda *_: zeros, pipeline_mode=pl.Buffered(1))


def _params(n_grid_dims):
    return pltpu.CompilerParams(dimension_semantics=("arbitrary",) * n_grid_dims,
                                vmem_limit_bytes=VMEM_LIMIT)


def _mod_body(c_ref, w_ref, b_ref, o_ref):
    c = c_ref[...]
    cond = c * jax.nn.sigmoid(c)
    o_ref[...] = jnp.dot(cond, w_ref[...], preferred_element_type=F32,
                         precision=lax.Precision.HIGHEST) + b_ref[...]


def _modulation(c, w, b, block_n=2048):
    bsz, d = c.shape
    n = w.shape[1]
    return pl.pallas_call(
        _mod_body,
        grid=(n // block_n,),
        in_specs=[pl.BlockSpec((bsz, d), lambda j: (0, 0)),
                  pl.BlockSpec((d, block_n), lambda j: (0, j)),
                  pl.BlockSpec((1, block_n), lambda j: (0, j))],
        out_specs=pl.BlockSpec((bsz, block_n), lambda j: (0, j)),
        out_shape=jax.ShapeDtypeStruct((bsz, n), F32),
        compiler_params=_params(1),
    )(c, w, b.reshape(1, n))


def _disc_body(lre_ref, lim_ref, ldt_ref, bre_ref, bim_ref, are_ref, aim_ref, obre_ref, obim_ref):
    lre = lre_ref[...]
    lim = lim_ref[...]
    dt = jnp.exp(ldt_ref[...])
    mag = jnp.exp(lre * dt)
    are = mag * jnp.cos(lim * dt)
    aim = mag * jnp.sin(lim * dt)
    are_ref[...] = are
    aim_ref[...] = aim
    nre = are - 1.0
    den = lre * lre + lim * lim
    cre = (nre * lre + aim * lim) / den
    cim = (aim * lre - nre * lim) / den
    bre = bre_ref[...]
    bim = bim_ref[...]
    obre_ref[...] = cre * bre - cim * bim
    obim_ref[...] = cre * bim + cim * bre


def _discretize(lam_re, lam_im, log_dt, b_re_t, b_im_t):
    g, p = lam_re.shape
    h = b_re_t.shape[1]
    are, aim, bbar_re, bbar_im = pl.pallas_call(
        _disc_body,
        out_shape=(jax.ShapeDtypeStruct((g, 1, p), F32), jax.ShapeDtypeStruct((g, 1, p), F32),
                   jax.ShapeDtypeStruct((g, h, p), F32), jax.ShapeDtypeStruct((g, h, p), F32)),
    )(lam_re.reshape(g, 1, p), lam_im.reshape(g, 1, p), log_dt.reshape(g, 1, 1), b_re_t, b_im_t)
    return are.reshape(g, p), aim.reshape(g, p), bbar_re, bbar_im


def _rope_body(pos_ref, f_ref, t1_ref, t2_ref):
    ang = f_ref[...] * pos_ref[...]
    cos = jnp.cos(ang)
    sin = jnp.sin(ang)
    half = QK_ROPE // 2
    seq = ang.shape[1]
    t1_ref[:QK_NOPE, :] = jnp.ones((QK_NOPE, seq), F32)
    t2_ref[:QK_NOPE, :] = jnp.zeros((QK_NOPE, seq), F32)
    for r0 in (QK_NOPE, QK_NOPE + half):
        t1_ref[r0:r0 + half, :] = cos
        t2_ref[r0:r0 + half, :] = sin
    pad = HEAD_PAD - QK_NOPE - QK_ROPE
    t1_ref[HEAD_PAD - pad:, :] = jnp.zeros((pad, seq), F32)
    t2_ref[HEAD_PAD - pad:, :] = jnp.zeros((pad, seq), F32)


def _rope_tables(posf, inv_freq):
    bsz, seq = posf.shape
    nf = inv_freq.shape[0]
    out = jax.ShapeDtypeStruct((bsz, HEAD_PAD, seq), F32)
    return pl.pallas_call(
        _rope_body,
        grid=(bsz,),
        in_specs=[pl.BlockSpec((None, 1, seq), lambda b: (b, 0, 0)),
                  pl.BlockSpec((nf, 1), lambda b: (0, 0))],
        out_specs=(pl.BlockSpec((None, HEAD_PAD, seq), lambda b: (b, 0, 0)),
                   pl.BlockSpec((None, HEAD_PAD, seq), lambda b: (b, 0, 0))),
        out_shape=(out, out),
        compiler_params=_params(1),
    )(posf.reshape(bsz, 1, seq), inv_freq.reshape(nf, 1))


def _inproj_body(x_ref, mod_ref, sw_ref, g1_ref, win_ref, qg_ref, wuq_ref, kvg_ref, wk_ref, wv_ref,
                 t1_ref, t2_ref, u_ref, q_ref, k_ref, v_ref, *, d_model, d_ssm, q_lora, kv_lora, scale):
    col_scale = g1_ref[...] * (1.0 + mod_ref[:, d_model:2 * d_model])
    s2 = d_ssm + q_lora
    s3 = s2 + kv_lora
    tm = x_ref.shape[0]
    part = tm // IN_PROJ_PARTS
    row = lax.broadcasted_iota(jnp.int32, (v_ref.shape[1], 1), 0)
    ones_rows = (row % V_ROWS == V_HEAD).astype(F32)
    for r0 in range(0, tm, part):
        rows = slice(r0, r0 + part)
        x = x_ref[rows, :]
        r_x = lax.rsqrt(jnp.mean(x * x, axis=-1, keepdims=True) + EPS)
        proj = _dot((x * col_scale).astype(BF16), win_ref[...]) * r_x + sw_ref[...]
        u_ref[rows, :] = proj[:, :d_ssm].astype(BF16)

        ql = proj[:, d_ssm:s2]
        r_q = lax.rsqrt(jnp.mean(ql * ql, axis=-1, keepdims=True) + EPS) * scale
        q = _dot((ql * qg_ref[...]).astype(BF16), wuq_ref[...])
        kvn = (_rms(proj[:, s2:s3]) * kvg_ref[...]).astype(BF16)
        kn = _dot(kvn, wk_ref[...])
        v_t = lax.dot_general(wv_ref[...], kvn, (((1,), (1,)), ((), ())), preferred_element_type=F32)
        v_ref[r0 // part] = (v_t + ones_rows).astype(BF16)

        t1 = t1_ref[:, rows].T
        t2 = t2_ref[:, rows].T
        kr = proj[:, s3:s3 + HEAD_PAD]
        kr = kr * t1 + pltpu.roll(kr, HEAD_PAD - QK_ROPE, 1) * t2
        for hh in range(N_HEADS):
            sl = slice(hh * HEAD_PAD, (hh + 1) * HEAD_PAD)
            qh = q[:, sl]
            qh = qh * t1 + pltpu.roll(qh, HEAD_PAD - QK_ROPE, 1) * t2
            q_ref[rows, sl] = (qh * r_q).astype(BF16)
            k_ref[rows, sl] = (kn[:, sl] + kr).astype(BF16)


def _shift_proj_body(s_ref, w_ref, o_ref):
    o_ref[...] = _dot(s_ref[...].astype(BF16), w_ref[...])


def _shift_proj(shift, w):
    out = pl.pallas_call(
        _shift_proj_body,
        out_shape=jax.ShapeDtypeStruct((shift.shape[0], w.shape[1]), F32),
        compiler_params=pltpu.CompilerParams(vmem_limit_bytes=VMEM_LIMIT),
    )(shift, w)
    return out.reshape(shift.shape[0], 1, w.shape[1])


def _in_proj(x, mod3, sw3, g1, win, qg, wuq, kvg, wk, wv, t1, t2, *, tm, d_ssm, q_lora, kv_lora):
    bsz, seq, d = x.shape
    tkv = tm
    tm = IN_PROJ_PARTS * tkv
    nt = seq // tm
    hp = N_HEADS * HEAD_PAD
    scale = float((QK_NOPE + QK_ROPE) ** -0.5 * math.log2(math.e))
    const = lambda b, i: (0, 0)
    body = functools.partial(_inproj_body, d_model=d, d_ssm=d_ssm, q_lora=q_lora, kv_lora=kv_lora,
                             scale=scale)
    return pl.pallas_call(
        body,
        grid=(bsz, nt),
        in_specs=[pl.BlockSpec((None, tm, d), lambda b, i: (b, i, 0)),
                  pl.BlockSpec((None, 1, mod3.shape[2]), lambda b, i: (b, 0, 0)),
                  pl.BlockSpec((None, 1, sw3.shape[2]), lambda b, i: (b, 0, 0)),
                  pl.BlockSpec(g1.shape, const),
                  pl.BlockSpec(win.shape, const),
                  pl.BlockSpec(qg.shape, const),
                  pl.BlockSpec(wuq.shape, const),
                  pl.BlockSpec(kvg.shape, const),
                  pl.BlockSpec(wk.shape, const),
                  pl.BlockSpec(wv.shape, const),
                  pl.BlockSpec((None, HEAD_PAD, tm), lambda b, i: (b, 0, i)),
                  pl.BlockSpec((None, HEAD_PAD, tm), lambda b, i: (b, 0, i))],
        out_specs=(pl.BlockSpec((tm, d_ssm), lambda b, i: (i, b)),
                   pl.BlockSpec((None, tm, hp), lambda b, i: (b, i, 0)),
                   pl.BlockSpec((None, tm, hp), lambda b, i: (b, i, 0)),
                   pl.BlockSpec((None, IN_PROJ_PARTS, wv.shape[0], tkv), lambda b, i: (b, i, 0, 0))),
        out_shape=(jax.ShapeDtypeStruct((seq, bsz * d_ssm), BF16),
                   jax.ShapeDtypeStruct((bsz, seq, hp), BF16),
                   jax.ShapeDtypeStruct((bsz, seq, hp), BF16),
                   jax.ShapeDtypeStruct((bsz, seq // tkv, wv.shape[0], tkv), BF16)),
        compiler_params=_params(2),
    )(x, mod3, sw3, g1, win, qg, wuq, kvg, wk, wv, t1, t2)


S5_SPLITS = 4


def _s5_body(u_ref, bblk_ref, are_ref, aim_ref, cblk_ref, d_ref, wglu_ref, g_ref, y_ref,
             bu_scr, x_scr, st_scr, *, lt, bsz, d_ssm):
    rows = lt * bsz
    n_split = bblk_ref.shape[0]
    sp_in = bblk_ref.shape[1]
    sp_st = bblk_ref.shape[2]
    pair = 2 * LANE
    pairs_per_split = sp_st // pair

    @pl.when(pl.program_id(0) == 0)
    def _():
        st_scr[...] = jnp.zeros_like(st_scr)

    u = u_ref[...].reshape(rows, d_ssm)
    ys = []
    for sp in range(n_split):
        bu_scr[:, sp * sp_st:(sp + 1) * sp_st] = _dot(u[:, sp * sp_in:(sp + 1) * sp_in], bblk_ref[sp])

        ms = range(sp * pairs_per_split, (sp + 1) * pairs_per_split)
        ar = [jnp.broadcast_to(are_ref[m:m + 1, :], (bsz, LANE)) for m in ms]
        ai = [jnp.broadcast_to(aim_ref[m:m + 1, :], (bsz, LANE)) for m in ms]
        state = [(st_scr[:, m * pair:m * pair + LANE], st_scr[:, m * pair + LANE:(m + 1) * pair])
                 for m in ms]
        for t in range(lt):
            r0 = t * bsz
            for j, m in enumerate(ms):
                xr, xi = state[j]
                c0 = m * pair
                nr = ar[j] * xr - ai[j] * xi + bu_scr[r0:r0 + bsz, c0:c0 + LANE]
                ni = ar[j] * xi + ai[j] * xr + bu_scr[r0:r0 + bsz, c0 + LANE:c0 + pair]
                x_scr[r0:r0 + bsz, c0:c0 + LANE] = nr.astype(BF16)
                x_scr[r0:r0 + bsz, c0 + LANE:c0 + pair] = ni.astype(BF16)
                state[j] = (nr, ni)
        for j, m in enumerate(ms):
            st_scr[:, m * pair:m * pair + LANE] = state[j][0]
            st_scr[:, m * pair + LANE:(m + 1) * pair] = state[j][1]

        ys.append(_dot(x_scr[:, sp * sp_st:(sp + 1) * sp_st], cblk_ref[sp]))
    y = jnp.concatenate(ys, axis=1) + d_ref[...] * u.astype(F32)
    y = jax.nn.gelu(y)
    z = _dot(y.astype(BF16), wglu_ref[...])
    o = z[:, :d_ssm] * jax.nn.sigmoid(z[:, d_ssm:])
    o = _rms(o) * g_ref[...]
    y_ref[...] = o.astype(BF16).reshape(lt, bsz, d_ssm)


def _s5(u_tm, bblk, are, aim, cblk, dvec, wglu, g, *, lt):
    seq, bsz, d_ssm = u_tm.shape
    n_state = bblk.shape[0] * bblk.shape[2]
    rows = lt * bsz
    c2 = lambda i: (0, 0)
    c3 = lambda i: (0, 0, 0)
    body = functools.partial(_s5_body, lt=lt, bsz=bsz, d_ssm=d_ssm)
    return pl.pallas_call(
        body,
        grid=(seq // lt,),
        in_specs=[pl.BlockSpec((lt, bsz, d_ssm), lambda i: (i, 0, 0)),
                  pl.BlockSpec(bblk.shape, c3),
                  pl.BlockSpec(are.shape, c2),
                  pl.BlockSpec(aim.shape, c2),
                  pl.BlockSpec(cblk.shape, c3),
                  pl.BlockSpec(dvec.shape, c2),
                  pl.BlockSpec(wglu.shape, c2),
                  pl.BlockSpec(g.shape, c2)],
        out_specs=pl.BlockSpec((lt, bsz, d_ssm), lambda i: (i, 0, 0)),
        out_shape=jax.ShapeDtypeStruct((seq, bsz, d_ssm), BF16),
        scratch_shapes=[pltpu.VMEM((rows, n_state), F32),
                        pltpu.VMEM((rows, n_state), BF16),
                        pltpu.VMEM((bsz, n_state), F32)],
        compiler_params=_params(1),
    )(u_tm, bblk, are, aim, cblk, dvec, wglu, g)


ATTN_HEAD_GROUP = 8


def _attn_body(q_ref, k_ref, vt_ref, g_ref, o_ref, s_scr, m_scr, acc_scr, *, tq, group):
    qi = pl.program_id(1)
    half = tq // 2
    contract_last = (((1,), (1,)), ((), ()))
    diag_parts = ((half, 0), (tq, half))

    for h0 in range(0, N_HEADS, group):
        heads = list(range(h0, h0 + group))

        def diag_block(heads=heads):
            k0 = pl.multiple_of(qi * tq, tq)
            for j, hh in enumerate(heads):
                sl = slice(hh * HEAD_PAD, (hh + 1) * HEAD_PAD)
                for nk, q0 in diag_parts:
                    s = lax.dot_general(k_ref[pl.ds(k0, nk), sl], q_ref[q0:q0 + half, sl], contract_last,
                                        preferred_element_type=F32)
                    key = lax.broadcasted_iota(jnp.int32, (nk, half), 0)
                    qry = lax.broadcasted_iota(jnp.int32, (nk, half), 1) + q0
                    s_scr[j, :nk, q0:q0 + half] = jnp.where(key <= qry, s, MASK_VALUE)
            for j, hh in enumerate(heads):
                sl = slice(hh * HEAD_PAD, (hh + 1) * HEAD_PAD)
                for nk, q0 in diag_parts:
                    s = s_scr[j, :nk, q0:q0 + half]
                    m_new = jnp.max(s, axis=0, keepdims=True)
                    p = jnp.exp2(s - m_new).astype(BF16)
                    vsl = slice(hh * V_ROWS, (hh + 1) * V_ROWS)
                    acc_scr[hh, :, q0:q0 + half] = _dot(vt_ref[qi, vsl, :nk], p)
                    m_scr[hh, :, q0:q0 + half] = m_new

        def full_block(ki, heads=heads):
            k0 = pl.multiple_of(ki * tq, tq)
            for j, hh in enumerate(heads):
                sl = slice(hh * HEAD_PAD, (hh + 1) * HEAD_PAD)
                s_scr[j] = lax.dot_general(k_ref[pl.ds(k0, tq), sl], q_ref[:, sl], contract_last,
                                           preferred_element_type=F32)
            for j, hh in enumerate(heads):
                sl = slice(hh * HEAD_PAD, (hh + 1) * HEAD_PAD)
                s = s_scr[j]
                m_prev = m_scr[hh]
                m_new = jnp.maximum(m_prev, jnp.max(s, axis=0, keepdims=True))
                p = jnp.exp2(s - m_new).astype(BF16)
                pv = _dot(vt_ref[ki, hh * V_ROWS:(hh + 1) * V_ROWS, :], p)
                acc_scr[hh] = jnp.exp2(m_prev - m_new) * acc_scr[hh] + pv
                m_scr[hh] = m_new

        diag_block()

        def kv_step(ki, carry, full_block=full_block):
            full_block(ki)
            return carry

        lax.fori_loop(0, qi, kv_step, 0)

    outs = []
    for hh in range(N_HEADS):
        acc = acc_scr[hh]
        outs.append(acc[:V_HEAD] / acc[V_HEAD:V_HEAD + 1])
    o_t = jnp.concatenate(outs, axis=0)
    o_t = o_t * lax.rsqrt(jnp.mean(o_t * o_t, axis=0, keepdims=True) + EPS)
    o_ref[...] = (o_t.T * g_ref[...]).astype(BF16)


def _attention(q, k, vt, g, *, tq, group):
    bsz, seq, hp = q.shape
    dv = N_HEADS * V_HEAD
    body = functools.partial(_attn_body, tq=tq, group=group)
    return pl.pallas_call(
        body,
        grid=(bsz, seq // tq),
        in_specs=[pl.BlockSpec((None, tq, hp), lambda b, i: (b, i, 0)),
                  pl.BlockSpec((None, seq, hp), lambda b, i: (b, 0, 0)),
                  pl.BlockSpec((None,) + vt.shape[1:], lambda b, i: (b, 0, 0, 0)),
                  pl.BlockSpec(g.shape, lambda b, i: (0, 0))],
        out_specs=pl.BlockSpec((None, tq, dv), lambda b, i: (b, i, 0)),
        out_shape=jax.ShapeDtypeStruct((bsz, seq, dv), BF16),
        scratch_shapes=[pltpu.VMEM((group, tq, tq), F32),
                        pltpu.VMEM((N_HEADS, 1, tq), F32),
                        pltpu.VMEM((N_HEADS, V_ROWS, tq), F32)],
        compiler_params=_params(2),
    )(q, k, vt, g)


FF_CHUNK = 1024
OUT_FFN_PARTS = 2


def _outffn_body(x_ref, ys_ref, ya_ref, mod_ref, fmod_ref, wos_ref, woa_ref, g2_ref, w1_ref, w2_ref,
                 gf_ref, o_ref, *, d_model):
    d = d_model
    mod = mod_ref[...]
    gate1 = mod[:, 2 * d:3 * d]
    shift2 = mod[:, 3 * d:4 * d]
    scale2 = mod[:, 4 * d:5 * d]
    gate2 = mod[:, 5 * d:6 * d]
    fmod = fmod_ref[...]
    fshift = fmod[:, :d]
    fscale = fmod[:, d:2 * d]
    d_ff = w1_ref.shape[1]
    tm = x_ref.shape[0]
    part = tm // OUT_FFN_PARTS

    def prologue(rows):
        mix = _dot(ys_ref[rows, :], wos_ref[...]) + _dot(ya_ref[rows, :], woa_ref[...])
        x1 = x_ref[rows, :] + gate1 * mix
        o_ref[rows, :] = x1
        return (_rms(x1) * g2_ref[...] * (1.0 + scale2) + shift2).astype(BF16)

    def ff_chunk(h, c0):
        a = jnp.maximum(_dot(h, w1_ref[:, c0:c0 + FF_CHUNK]), 0.0)
        return _dot((a * a).astype(BF16), w2_ref[c0:c0 + FF_CHUNK, :])

    def epilogue(rows, ff):
        x2 = o_ref[rows, :] + gate2 * ff
        o_ref[rows, :] = _rms(x2) * gf_ref[...] * (1.0 + fscale) + fshift

    ranges = [slice(r0, r0 + part) for r0 in range(0, tm, part)]
    chunks = list(range(0, d_ff, FF_CHUNK))
    h_cur = prologue(ranges[0])
    pending = None
    for idx, rows in enumerate(ranges):
        ff = ff_chunk(h_cur, chunks[0])
        if pending is not None:
            epilogue(*pending)
        h_next = prologue(ranges[idx + 1]) if idx + 1 < len(ranges) else None
        for c0 in chunks[1:]:
            ff = ff + ff_chunk(h_cur, c0)
        pending = (rows, ff)
        h_cur = h_next
    epilogue(*pending)


def _out_ffn(x, ys2d, ya, mod3, fmod3, wos, woa, g2, w1, w2, gf, *, tm):
    bsz, seq, d = x.shape
    d_half = ya.shape[2]
    body = functools.partial(_outffn_body, d_model=d)
    return pl.pallas_call(
        body,
        grid=(bsz, seq // tm),
        in_specs=[pl.BlockSpec((None, tm, d), lambda b, i: (b, i, 0)),
                  pl.BlockSpec((tm, d_half), lambda b, i: (i, b)),
                  pl.BlockSpec((None, tm, d_half), lambda b, i: (b, i, 0)),
                  pl.BlockSpec((None, 1, mod3.shape[2]), lambda b, i: (b, 0, 0)),
                  pl.BlockSpec((None, 1, fmod3.shape[2]), lambda b, i: (b, 0, 0)),
                  _resident(wos.shape),
                  _resident(woa.shape),
                  _resident(g2.shape),
                  _resident(w1.shape),
                  _resident(w2.shape),
                  _resident(gf.shape)],
        out_specs=pl.BlockSpec((None, tm, d), lambda b, i: (b, i, 0)),
        out_shape=jax.ShapeDtypeStruct((bsz, seq, d), F32),
        compiler_params=_params(2),
    )(x, ys2d, ya, mod3, fmod3, wos, woa, g2, w1, w2, gf)


def _pad_heads(w, head_in, pieces, width=HEAD_PAD):
    k = w.shape[0]
    w3 = w.reshape(k, N_HEADS, head_in)
    cols = []
    for piece in pieces:
        if isinstance(piece, int):
            cols.append(jnp.zeros((k, N_HEADS, piece), w.dtype))
        else:
            cols.append(piece(w3))
    out = jnp.concatenate(cols, axis=2)
    assert out.shape[2] == width
    return out.reshape(k, N_HEADS * width)


def _rot_half(wr):
    half = QK_ROPE // 2
    return jnp.concatenate([-wr[..., half:], wr[..., :half]], axis=-1)


def _s5_block_weights(bbar_re, bbar_im, c_re, c_im):
    g, h, p = bbar_re.shape
    gs = g // S5_SPLITS
    eye = jnp.eye(gs, dtype=F32)
    bb = jnp.stack([bbar_re, bbar_im], axis=2).reshape(S5_SPLITS, gs, h, 1, 2, p)
    b6 = bb * eye.reshape(1, gs, 1, gs, 1, 1)
    b6 = b6.reshape(S5_SPLITS, gs, h, gs // 2, 2, 2, p).transpose(0, 1, 2, 3, 5, 4, 6)
    bblk = b6.reshape(S5_SPLITS, gs * h, gs * 2 * p)
    cc = jnp.stack([c_re, -c_im], axis=2).reshape(S5_SPLITS, 1, gs, h, 2, p)
    c6 = cc * eye.reshape(1, gs, gs, 1, 1, 1)
    c6 = c6.reshape(S5_SPLITS, gs // 2, 2, gs, h, 2, p).transpose(0, 1, 5, 2, 6, 3, 4)
    cblk = c6.reshape(S5_SPLITS, gs * 2 * p, gs * h)
    return bblk.astype(BF16), cblk.astype(BF16)


def _forward(x, c, positions, ada_w, ada_b, norm1_g, w_in, ssm_lambda_re, ssm_lambda_im,
             ssm_b_re, ssm_b_im, ssm_c_re, ssm_c_im, ssm_d, ssm_log_dt, w_glu,
             q_norm_g, w_uq, kv_norm_g, w_ukv, ssm_out_g, attn_out_g, w_out,
             norm2_g, w_ff1, w_ff2, final_ada_w, final_ada_b, final_norm_g, *, tm, tq, lt):
    bsz, seq, d = x.shape
    depth = ada_w.shape[0]
    assert depth == 1, "the fused epilogue applies the final norm right after the only layer"
    assert tm == tq, "in_proj emits one V^T tile per attention key block"
    d_ssm = w_glu.shape[1]
    q_lora = w_uq.shape[1]
    kv_lora = w_ukv.shape[1]
    s2 = d_ssm + q_lora
    s3 = s2 + kv_lora

    inv_freq = ROPE_BASE ** (-jnp.arange(0, QK_ROPE, 2, dtype=F32) / QK_ROPE)
    t1, t2 = _rope_tables(positions.astype(F32), inv_freq)

    fmod = _modulation(c, final_ada_w, final_ada_b)
    fmod3 = fmod.reshape(bsz, 1, fmod.shape[1])

    for l in range(depth):
        mod = _modulation(c, ada_w[l], ada_b[l])
        mod3 = mod.reshape(bsz, 1, mod.shape[1])

        wi = w_in[l]
        wkr = wi[:, s3:]
        win = jnp.concatenate([wi[:, :s3], jnp.zeros((d, QK_NOPE), F32), wkr, _rot_half(wkr)],
                              axis=1).astype(BF16)
        hq = QK_NOPE + QK_ROPE
        wuq = _pad_heads(w_uq[l], hq, [lambda w3: w3[..., :hq],
                                       lambda w3: _rot_half(w3[..., QK_NOPE:])]).astype(BF16)
        wk = _pad_heads(w_ukv[l], QK_NOPE + V_HEAD,
                        [lambda w3: w3[..., :QK_NOPE], HEAD_PAD - QK_NOPE]).astype(BF16)
        wv = _pad_heads(w_ukv[l], QK_NOPE + V_HEAD,
                        [lambda w3: w3[..., QK_NOPE:], V_ROWS - V_HEAD], width=V_ROWS).T.astype(BF16)

        u2d, q, k, v = _in_proj(x, mod3, _shift_proj(mod[:, :d], win),
                                norm1_g[l].reshape(1, d), win, q_norm_g[l].reshape(1, q_lora),
                                wuq, kv_norm_g[l].reshape(1, kv_lora), wk, wv, t1, t2,
                                tm=tm, d_ssm=d_ssm, q_lora=q_lora, kv_lora=kv_lora)

        are, aim, bbar_re, bbar_im = _discretize(
            ssm_lambda_re[l], ssm_lambda_im[l], ssm_log_dt[l],
            jnp.transpose(ssm_b_re[l], (0, 2, 1)), jnp.transpose(ssm_b_im[l], (0, 2, 1)))
        bblk, cblk = _s5_block_weights(bbar_re, bbar_im, ssm_c_re[l], ssm_c_im[l])
        n_pair = are.shape[0] // 2
        ys_tm = _s5(u2d.reshape(seq, bsz, d_ssm), bblk, are.reshape(n_pair, LANE), aim.reshape(n_pair, LANE),
                    cblk, ssm_d[l].reshape(1, d_ssm), w_glu[l].astype(BF16), ssm_out_g[l].reshape(1, d_ssm),
                    lt=lt)

        ya = _attention(q, k, v, attn_out_g[l].reshape(1, -1), tq=tq, group=ATTN_HEAD_GROUP)

        wo = w_out[l].astype(BF16)
        out = _out_ffn(x, ys_tm.reshape(seq, bsz * d_ssm), ya, mod3, fmod3, wo[:d_ssm], wo[d_ssm:],
                       norm2_g[l].reshape(1, d), w_ff1[l].astype(BF16), w_ff2[l].astype(BF16),
                       final_norm_g.reshape(1, d), tm=2 * tm)
        x = out
    return x


def kernel(x, c, positions, ada_w, ada_b, norm1_g, w_in, ssm_lambda_re, ssm_lambda_im, ssm_b_re, ssm_b_im, ssm_c_re, ssm_c_im, ssm_d, ssm_log_dt, w_glu, q_norm_g, w_uq, kv_norm_g, w_ukv, ssm_out_g, attn_out_g, w_out, norm2_g, w_ff1, w_ff2, final_ada_w, final_ada_b, final_norm_g):
    return _forward(x, c, positions, ada_w, ada_b, norm1_g, w_in, ssm_lambda_re, ssm_lambda_im,
                    ssm_b_re, ssm_b_im, ssm_c_re, ssm_c_im, ssm_d, ssm_log_dt, w_glu,
                    q_norm_g, w_uq, kv_norm_g, w_ukv, ssm_out_g, attn_out_g, w_out,
                    norm2_g, w_ff1, w_ff2, final_ada_w, final_ada_b, final_norm_g,
                    tm=512, tq=512, lt=64)
```
